```python
import math
import jax
import jax.numpy as jnp
from jax import lax
import numpy as np

D_MODEL = 2048
BATCH = 32
SEQ = 256
DEPTH = 4
DEC_BATCH = 8
DEC_SEQ = 4096
PAST_LEN = 256

GRID_W = 64
N_EVEN = (DEPTH + 1) // 2
N_ODD = DEPTH // 2
EPS = 1e-6
A_HEADS = 8
QK_NOPE = 128
QK_ROPE = 64
V_HEAD = 128
D_CQ = 512
D_CKV = 512
ROPE_BASE = 10000.0
Q_BLOCK = 128
B_HEADS = 8
DK = 128
DV = 128
CONV_W = 3
CHUNK = 64
W_A = A_HEADS * V_HEAD
W_B = B_HEADS * DV
MIX_W = W_A + W_B
QKV_W = B_HEADS * (2 * DK + DV)
AB_W = 4 * B_HEADS
IN_EVEN = D_CQ + D_CKV + QK_ROPE + QKV_W + AB_W + MIX_W
EVEN_SPLITS = (D_CQ, D_CQ + D_CKV, D_CQ + D_CKV + QK_ROPE, D_CQ + D_CKV + QK_ROPE + QKV_W,
               D_CQ + D_CKV + QK_ROPE + QKV_W + AB_W)
POOL_W = D_MODEL
POOL_WINDOWS = (2, 4, 8, 16)
POOL_GROUPS = 4
POOL_G = POOL_W // POOL_GROUPS

kernel_name = 'hybrid_mla_gdn_pool_diffusion_step'


def rms_norm(x, g):
    xf = x.astype(jnp.float32)
    y = xf * lax.rsqrt(jnp.mean(xf * xf, axis=-1, keepdims=True) + EPS)
    return (y * g.astype(jnp.float32)).astype(x.dtype)


def l2_normalize(x):
    xf = x.astype(jnp.float32)
    return xf * lax.rsqrt(jnp.sum(xf * xf, axis=-1, keepdims=True) + EPS)


def modulation(cond, w, b):
    m = jnp.einsum('bd,de->be', jax.nn.silu(cond), w) + b
    return jnp.split(m[:, None, :], 3, axis=-1)


def axial_rope_tables(n_tokens):
    rows = n_tokens // GRID_W
    row_pos = jnp.repeat(jnp.arange(rows), GRID_W).astype(jnp.float32)
    col_pos = jnp.tile(jnp.arange(GRID_W), rows).astype(jnp.float32)
    half = QK_ROPE // 2
    inv_freq = ROPE_BASE ** (-jnp.arange(0, half, 2, dtype=jnp.float32) / half)
    ang = jnp.concatenate([row_pos[:, None] * inv_freq, col_pos[:, None] * inv_freq], axis=-1)
    return jnp.cos(ang), jnp.sin(ang)


def _rotate(x, cos, sin):
    x1, x2 = jnp.split(x, 2, axis=-1)
    return jnp.concatenate([x1 * cos - x2 * sin, x2 * cos + x1 * sin], axis=-1)


def apply_axial_rope(x, cos, sin):
    xr, xc = jnp.split(x.astype(jnp.float32), 2, axis=-1)
    cr, cc = jnp.split(cos, 2, axis=-1)
    sr, sc = jnp.split(sin, 2, axis=-1)
    return jnp.concatenate([_rotate(xr, cr, sr), _rotate(xc, cc, sc)], axis=-1).astype(x.dtype)


def blocked_attention(q, k, v):
    B, Lq, H, Dqk = q.shape
    nb = Lq // Q_BLOCK
    qb = jnp.moveaxis(q.reshape(B, nb, Q_BLOCK, H, Dqk), 1, 0)
    scale = Dqk ** -0.5

    def one_block(qi):
        s = jnp.einsum('bqhd,bkhd->bhqk', qi, k, preferred_element_type=jnp.float32) * scale
        p = jax.nn.softmax(s, axis=-1).astype(v.dtype)
        return jnp.einsum('bhqk,bkhd->bqhd', p, v)

    o = lax.map(one_block, qb)
    return jnp.moveaxis(o, 0, 1).reshape(B, Lq, H, v.shape[-1])


def short_conv(x, w):
    return lax.conv_general_dilated(x, w[:, None, :].astype(x.dtype), window_strides=(1,),
                                    padding=[(CONV_W // 2, CONV_W // 2)],
                                    dimension_numbers=('NWC', 'WIO', 'NWC'),
                                    feature_group_count=x.shape[-1])


def gated_delta_chunked(q, k, v, g, beta, s0):
    B, H, L, _ = q.shape
    n = L // CHUNK
    dv = v.shape[-1]

    def chunks(t):
        return t.reshape(B, H, n, CHUNK, *t.shape[3:])

    q, k, v, g, beta = chunks(q), chunks(k), chunks(v), chunks(g), chunks(beta)
    G = jnp.cumsum(g, axis=-1)
    lower = jnp.tril(jnp.ones((CHUNK, CHUNK), bool))
    strict = jnp.tril(jnp.ones((CHUNK, CHUNK), bool), -1)
    gamma = jnp.exp(jnp.where(lower, G[..., :, None] - G[..., None, :], -jnp.inf))
    kb = k * beta[..., None]
    a = jnp.einsum('bhncd,bhnsd->bhncs', kb, k) * jnp.where(strict, gamma, 0.0)
    eye = jnp.eye(CHUNK, dtype=jnp.float32)
    rhs = jnp.concatenate([v * beta[..., None], kb * jnp.exp(G)[..., None]], axis=-1)
    sol = lax.linalg.triangular_solve(eye + a, rhs, left_side=True, lower=True, unit_diagonal=True)
    u, w = sol[..., :dv], sol[..., dv:]
    intra = jnp.einsum('bhncd,bhnsd->bhncs', q, k) * gamma
    q_dec = q * jnp.exp(G)[..., None]
    k_dec = k * jnp.exp(G[..., -1:] - G)[..., None]
    chunk_decay = jnp.exp(G[..., -1])
    xs = (jnp.moveaxis(u, 2, 0), jnp.moveaxis(w, 2, 0), jnp.moveaxis(q_dec, 2, 0),
          jnp.moveaxis(k_dec, 2, 0), jnp.moveaxis(intra, 2, 0), jnp.moveaxis(chunk_decay, 2, 0))

    def step(S, inp):
        u_c, w_c, qd_c, kd_c, intra_c, dec_c = inp
        v_new = u_c - jnp.einsum('bhcd,bhde->bhce', w_c, S)
        o_c = jnp.einsum('bhcd,bhde->bhce', qd_c, S) + jnp.einsum('bhcs,bhse->bhce', intra_c, v_new)
        S = S * dec_c[..., None, None] + jnp.einsum('bhcd,bhce->bhde', kd_c, v_new)
        return S, o_c

    s_final, o = lax.scan(step, s0, xs)
    return jnp.moveaxis(o, 0, 2).reshape(B, H, L, dv), s_final


def gdn_bidirectional(q, k, v, g_f, beta_f, g_b, beta_b, s0_f, s0_b):
    o_f, s_f = gated_delta_chunked(q, k, v, g_f, beta_f, s0_f)
    flip = lambda t: jnp.flip(t, axis=2)
    o_b, s_b = gated_delta_chunked(flip(q), flip(k), flip(v), flip(g_b), flip(beta_b), s0_b)
    return o_f + flip(o_b), s_f, s_b


def even_layer_inputs(h, lw):
    B, L, _ = h.shape
    proj = jnp.einsum('bld,de->ble', h, lw['w_in'])
    c_q, c_kv, k_pe, qkv, ab, z = jnp.split(proj, EVEN_SPLITS, axis=-1)
    q = jnp.einsum('blc,ce->ble', rms_norm(c_q, lw['q_norm']), lw['w_uq']).reshape(B, L, A_HEADS, QK_NOPE + QK_ROPE)
    ckv = rms_norm(c_kv, lw['kv_norm'])
    qkv = jax.nn.silu(short_conv(qkv, lw['conv']))
    gq, gk, gv = jnp.split(qkv, [B_HEADS * DK, 2 * B_HEADS * DK], axis=-1)
    heads = lambda t, d: jnp.transpose(t.reshape(B, L, B_HEADS, d), (0, 2, 1, 3))
    gq = l2_normalize(heads(gq, DK)) * DK ** -0.5
    gk = l2_normalize(heads(gk, DK))
    gv = heads(gv, DV).astype(jnp.float32)
    ab = jnp.transpose(ab.astype(jnp.float32).reshape(B, L, 4, B_HEADS), (2, 0, 3, 1))
    a_log = lw['a_log'].astype(jnp.float32)
    dt_bias = lw['dt_bias'].astype(jnp.float32)
    g = -jnp.exp(a_log)[:, None, :, None] * jax.nn.softplus(ab[:2] + dt_bias[:, None, :, None])
    beta = jax.nn.sigmoid(ab[2:])
    return {'q_nope': q[..., :QK_NOPE], 'q_pe': q[..., QK_NOPE:], 'ckv': ckv, 'k_pe': k_pe,
            'gq': gq, 'gk': gk, 'gv': gv, 'g': g, 'beta': beta, 'z': z}


def mla_kv(ckv, k_pe, w_ukv):
    B, L, _ = ckv.shape
    kv = jnp.einsum('blc,ce->ble', ckv, w_ukv).reshape(B, L, A_HEADS, QK_NOPE + V_HEAD)
    k = jnp.concatenate([kv[..., :QK_NOPE], jnp.broadcast_to(k_pe[:, :, None, :], (B, L, A_HEADS, QK_ROPE))], axis=-1)
    return k, kv[..., QK_NOPE:]


def even_layer_output(attn, gdn_o, z, lw):
    B, L = z.shape[:2]
    gdn_o = rms_norm(jnp.transpose(gdn_o, (0, 2, 1, 3)), lw['o_norm']).reshape(B, L, W_B).astype(z.dtype)
    y = jnp.concatenate([attn.reshape(B, L, W_A).astype(z.dtype), gdn_o], axis=-1) * jax.nn.silu(z)
    return jnp.einsum('ble,ed->bld', y, lw['w_out'])


def even_context(h, lw):
    B = h.shape[0]
    p = even_layer_inputs(h, lw)
    q = jnp.concatenate([p['q_nope'], p['q_pe']], axis=-1)
    k, v = mla_kv(p['ckv'], p['k_pe'], lw['w_ukv'])
    attn = blocked_attention(q, k, v)
    s0 = jnp.zeros((B, B_HEADS, DK, DV), jnp.float32)
    o, s_f, s_b = gdn_bidirectional(p['gq'], p['gk'], p['gv'], p['g'][0], p['beta'][0], p['g'][1], p['beta'][1], s0, s0)
    return even_layer_output(attn, o, p['z'], lw), p['ckv'], p['k_pe'], s_f, s_b


def even_latent(h, ckv_ctx, kpe_ctx, sf_ctx, sb_ctx, rope_cos, rope_sin, lw):
    p = even_layer_inputs(h, lw)
    q_pe = apply_axial_rope(p['q_pe'], rope_cos[:, None, :], rope_sin[:, None, :])
    k_pe = apply_axial_rope(p['k_pe'], rope_cos, rope_sin)
    q = jnp.concatenate([p['q_nope'], q_pe], axis=-1)
    k_lat, v_lat = mla_kv(p['ckv'], k_pe, lw['w_ukv'])
    k_ctx, v_ctx = mla_kv(ckv_ctx.astype(h.dtype), kpe_ctx.astype(h.dtype), lw['w_ukv'])
    k = jnp.concatenate([k_ctx, k_lat], axis=1)
    v = jnp.concatenate([v_ctx, v_lat], axis=1)
    attn = blocked_attention(q, k, v)
    o, _, _ = gdn_bidirectional(p['gq'], p['gk'], p['gv'], p['g'][0], p['beta'][0], p['g'][1], p['beta'][1],
                                sf_ctx.astype(jnp.float32), sb_ctx.astype(jnp.float32))
    return even_layer_output(attn, o, p['z'], lw)


def multiscale_pool(x):
    B, L, _ = x.shape
    xg = x.astype(jnp.float32).reshape(B, L, POOL_GROUPS, POOL_G)
    cs = jnp.concatenate([jnp.zeros((B, 1, POOL_GROUPS, POOL_G), jnp.float32), jnp.cumsum(xg, axis=1)], axis=1)
    t = jnp.arange(L)
    outs = []
    for gi, w in enumerate(POOL_WINDOWS):
        lo = jnp.clip(t - w // 2, 0, L)
        hi = jnp.clip(t + (w - w // 2), 0, L)
        mean = (cs[:, hi, gi] - cs[:, lo, gi]) / (hi - lo).astype(jnp.float32)[:, None]
        outs.append(mean - xg[:, :, gi])
    return jnp.stack(outs, axis=2).astype(x.dtype)


def pool_mixer(h, w_in, w_pool, pool_scale, w_out):
    B, L, _ = h.shape
    pin, z = jnp.split(jnp.einsum('bld,de->ble', h, w_in), 2, axis=-1)
    pooled = multiscale_pool(pin)
    mixed = jnp.einsum('blgc,gce->blge', pooled, w_pool).reshape(B, L, POOL_W) * pool_scale
    return jnp.einsum('ble,ed->bld', mixed * jax.nn.silu(z), w_out)


def setup_inputs(seed: int = 0) -> dict:
    key = jax.random.key(seed)
    ks = iter(jax.random.split(key, 40))
    f32 = jnp.float32

    def nrm(shape, s):
        return jax.random.normal(next(ks), shape, f32) * s

    dt = jnp.exp(jax.random.uniform(next(ks), (N_EVEN, 2, B_HEADS), f32, math.log(1e-3), math.log(1e-1)))
    return {
        'x_prompt': nrm((BATCH, SEQ, D_MODEL), 1.0),
        'x_sample': nrm((DEC_BATCH, DEC_SEQ, D_MODEL), 1.0),
        'cache_ckv': nrm((DEC_BATCH, N_EVEN, PAST_LEN, D_CKV), 1.0),
        'cache_kpe': nrm((DEC_BATCH, N_EVEN, PAST_LEN, QK_ROPE), 1.0),
        'state_fwd': nrm((DEC_BATCH, N_EVEN, B_HEADS, DK, DV), 0.1),
        'state_bwd': nrm((DEC_BATCH, N_EVEN, B_HEADS, DK, DV), 0.1),
        'c': nrm((DEC_BATCH, D_MODEL), 1.0),
        'c_ctx': nrm((D_MODEL,), 1.0),
        'ln_e': 1.0 + nrm((N_EVEN, D_MODEL), 0.02),
        'mod_w_e': nrm((N_EVEN, D_MODEL, 3 * D_MODEL), 0.5 * D_MODEL ** -0.5),
        'mod_b_e': nrm((N_EVEN, 3 * D_MODEL), 0.02),
        'w_in_e': nrm((N_EVEN, D_MODEL, IN_EVEN), D_MODEL ** -0.5),
        'q_norm_e': 1.0 + nrm((N_EVEN, D_CQ), 0.02),
        'kv_norm_e': 1.0 + nrm((N_EVEN, D_CKV), 0.02),
        'w_uq_e': nrm((N_EVEN, D_CQ, A_HEADS * (QK_NOPE + QK_ROPE)), D_CQ ** -0.5),
        'w_ukv_e': nrm((N_EVEN, D_CKV, A_HEADS * (QK_NOPE + V_HEAD)), D_CKV ** -0.5),
        'conv_e': nrm((N_EVEN, CONV_W, QKV_W), CONV_W ** -0.5),
        'a_log_e': jnp.log(jax.random.uniform(next(ks), (N_EVEN, 2, B_HEADS), f32, 1.0, 16.0)),
        'dt_bias_e': dt + jnp.log(-jnp.expm1(-dt)),
        'o_norm_e': 1.0 + nrm((N_EVEN, DV), 0.02),
        'w_out_e': nrm((N_EVEN, MIX_W, D_MODEL), MIX_W ** -0.5),
        'ln_o': 1.0 + nrm((N_ODD, D_MODEL), 0.02),
        'mod_w_o': nrm((N_ODD, D_MODEL, 3 * D_MODEL), 0.5 * D_MODEL ** -0.5),
        'mod_b_o': nrm((N_ODD, 3 * D_MODEL), 0.02),
        'w_in_o': nrm((N_ODD, D_MODEL, 2 * POOL_W), D_MODEL ** -0.5),
        'w_pool_o': nrm((N_ODD, POOL_GROUPS, POOL_G, POOL_G), POOL_G ** -0.5),
        'pool_scale_o': 1.0 + nrm((N_ODD, POOL_W), 0.1),
        'w_out_o': nrm((N_ODD, POOL_W, D_MODEL), POOL_W ** -0.5),
        'final_norm': 1.0 + nrm((D_MODEL,), 0.02),
    }


def reference(x_prompt, x_sample, cache_ckv, cache_kpe, state_fwd, state_bwd, c, c_ctx,
              ln_e, mod_w_e, mod_b_e, w_in_e, q_norm_e, kv_norm_e, w_uq_e, w_ukv_e, conv_e,
              a_log_e, dt_bias_e, o_norm_e, w_out_e,
              ln_o, mod_w_o, mod_b_o, w_in_o, w_pool_o, pool_scale_o, w_out_o, final_norm):
    rope_cos, rope_sin = axial_rope_tables(x_sample.shape[1])
    cond_ctx = c_ctx[None, :]
    xp, xs = x_prompt, x_sample
    ckv_out, kpe_out, sf_out, sb_out = [], [], [], []
    for layer in range(DEPTH):
        i = layer // 2
        if layer % 2 == 0:
            lw = {'w_in': w_in_e[i], 'q_norm': q_norm_e[i], 'kv_norm': kv_norm_e[i], 'w_uq': w_uq_e[i],
                  'w_ukv': w_ukv_e[i], 'conv': conv_e[i], 'a_log': a_log_e[i], 'dt_bias': dt_bias_e[i],
                  'o_norm': o_norm_e[i], 'w_out': w_out_e[i]}
            sh_p, sc_p, gt_p = modulation(cond_ctx, mod_w_e[i], mod_b_e[i])
            sh_s, sc_s, gt_s = modulation(c, mod_w_e[i], mod_b_e[i])
            hp = rms_norm(xp, ln_e[i]) * (1.0 + sc_p) + sh_p
            hs = rms_norm(xs, ln_e[i]) * (1.0 + sc_s) + sh_s
            out_p, ckv, kpe, s_f, s_b = even_context(hp, lw)
            out_s = even_latent(hs, cache_ckv[:, i], cache_kpe[:, i], state_fwd[:, i], state_bwd[:, i],
                                rope_cos, rope_sin, lw)
            ckv_out.append(ckv)
            kpe_out.append(kpe)
            sf_out.append(s_f)
            sb_out.append(s_b)
        else:
            sh_p, sc_p, gt_p = modulation(cond_ctx, mod_w_o[i], mod_b_o[i])
            sh_s, sc_s, gt_s = modulation(c, mod_w_o[i], mod_b_o[i])
            hp = rms_norm(xp, ln_o[i]) * (1.0 + sc_p) + sh_p
            hs = rms_norm(xs, ln_o[i]) * (1.0 + sc_s) + sh_s
            out_p = pool_mixer(hp, w_in_o[i], w_pool_o[i], pool_scale_o[i], w_out_o[i])
            out_s = pool_mixer(hs, w_in_o[i], w_pool_o[i], pool_scale_o[i], w_out_o[i])
        xp = xp + gt_p * out_p
        xs = xs + gt_s * out_s
    y_prompt = rms_norm(xp, final_norm)
    y_sample = rms_norm(xs, final_norm)
    new_cache_ckv = jnp.stack(ckv_out, axis=1)
    new_cache_kpe = jnp.stack(kpe_out, axis=1)
    new_state_fwd = jnp.stack(sf_out, axis=1)
    new_state_bwd = jnp.stack(sb_out, axis=1)
    return (y_prompt, y_sample, new_cache_ckv, new_cache_kpe, new_state_fwd, new_state_bwd)
```

```python
import functools
import math

import jax
import jax.numpy as jnp
from jax import lax
from jax.experimental import pallas as pl
from jax.experimental.pallas import tpu as pltpu

F32 = jnp.float32
BF16 = jnp.bfloat16

EPS = 1e-6
GRID_W = 64
ROPE_BASE = 10000.0
A_HEADS = 8
QK_NOPE = 128
QK_ROPE = 64
V_HEAD = 128
D_CQ = 512
D_CKV = 512
B_HEADS = 8
DK = 128
DV = 128
CHUNK = 64
POOL_WINDOWS = (2, 4, 8, 16)
POOL_HALO = 8
QKV_W = B_HEADS * (2 * DK + DV)
AB_W = 4 * B_HEADS
LANES = 128
V7X_VMEM_BYTES = 64 * 1024 * 1024
VMEM_LIMIT = V7X_VMEM_BYTES - 8 * 1024 * 1024
COND_ROWS = 16

_C_CQ = 0
_C_CKV = D_CQ
_C_KPE = D_CQ + D_CKV
_C_KPE_SW = _C_KPE + LANES
_C_AB = _C_KPE_SW + LANES
_C_QKV = _C_AB + LANES
IN_W = _C_QKV + QKV_W


def _mm(a, b):
    return jnp.dot(a.astype(BF16), b.astype(BF16), preferred_element_type=F32)


def _mm_nt(a, b):
    return lax.dot_general(a.astype(BF16), b.astype(BF16), (((1,), (1,)), ((), ())),
                           preferred_element_type=F32)


def _mm_tn(a, b):
    return lax.dot_general(a.astype(BF16), b.astype(BF16), (((0,), (0,)), ((), ())),
                           preferred_element_type=F32)


def _split_bf16(x):
    hi = x.astype(BF16)
    return hi, (x - hi.astype(F32)).astype(BF16)


def _mm3(a, b):
    ah, al = _split_bf16(a)
    bh, bl = _split_bf16(b)
    dot = functools.partial(jnp.dot, preferred_element_type=F32)
    return (dot(ah, bl) + dot(al, bh)) + dot(ah, bh)


def _sigmoid(x):
    return 1.0 / (1.0 + jnp.exp(-x))


def _silu(x):
    return x * _sigmoid(x)


def _softplus(x):
    return jnp.maximum(x, 0.0) + jnp.log(1.0 + jnp.exp(-jnp.abs(x)))


def _rms(x, g):
    return x * lax.rsqrt(jnp.mean(x * x, axis=-1, keepdims=True) + EPS) * g


def _adaln(x, ln, sc, sh):
    return _rms(x, ln) * (1.0 + sc) + sh


def _params(*sem):
    return pltpu.CompilerParams(dimension_semantics=sem, vmem_limit_bytes=VMEM_LIMIT)


def _resident(shape):
    nd = len(shape)
    return pl.BlockSpec(shape, lambda *_: (0,) * nd, pipeline_mode=pl.Buffered(1))


def _mod_kernel(c_ref, w_ref, b_ref, o_ref):
    o_ref[0] = _mm(_silu(c_ref[...]), w_ref[0]) + b_ref[0]


def _modulation(cond, w, b):
    n, d, e = w.shape
    tn = 1024
    return pl.pallas_call(
        _mod_kernel,
        out_shape=jax.ShapeDtypeStruct((n, COND_ROWS, e), F32),
        grid=(n, e // tn),
        in_specs=[pl.BlockSpec((COND_ROWS, d), lambda l, j: (0, 0)),
                  pl.BlockSpec((1, d, tn), lambda l, j: (l, 0, j)),
                  pl.BlockSpec((1, 1, tn), lambda l, j: (l, 0, j))],
        out_specs=pl.BlockSpec((1, COND_ROWS, tn), lambda l, j: (l, 0, j)),
        compiler_params=_params("parallel", "parallel"),
        name="modulation",
    )(cond, w, b.reshape(n, 1, e))


def _even_in_kernel(rope, emit_cache, x_ref, sh_ref, sc_ref, ln_ref, w_ref, qg_ref, kg_ref, wuq_ref, wukv_ref, *rest):
    if rope:
        cos_ref, sin_ref, *outs = rest
    else:
        outs = rest
    if emit_cache:
        q_ref, kv_ref, kper_ref, qkv_ref, ab_ref, ckv_ref, kpe_ref = outs
    else:
        q_ref, kv_ref, kper_ref, qkv_ref, ab_ref = outs
    h = _adaln(x_ref[0], ln_ref[...], sc_ref[0], sh_ref[0])
    p = _mm(h, w_ref[...])
    kpe = p[:, _C_KPE:_C_KPE + QK_ROPE]
    ckv = _rms(p[:, _C_CKV:_C_CKV + D_CKV], kg_ref[...])
    qq = _mm(_rms(p[:, _C_CQ:_C_CQ + D_CQ], qg_ref[...]), wuq_ref[...])
    n_nope = A_HEADS * QK_NOPE
    n_pe = A_HEADS * QK_ROPE
    qpe = qq[:, n_nope:n_nope + n_pe]
    if rope:
        cos = cos_ref[...]
        sin = sin_ref[...]
        kpe = kpe * cos[:, :QK_ROPE] + p[:, _C_KPE_SW:_C_KPE_SW + QK_ROPE] * sin[:, :QK_ROPE]
        reps = n_pe // LANES
        qpe = (qpe * jnp.concatenate([cos] * reps, axis=1)
               + qq[:, n_nope + n_pe:] * jnp.concatenate([sin] * reps, axis=1))
    for hd in range(A_HEADS):
        q_ref[0, hd, :, :QK_NOPE] = qq[:, hd * QK_NOPE:(hd + 1) * QK_NOPE].astype(BF16)
        q_ref[0, hd, :, QK_NOPE:] = qpe[:, hd * QK_ROPE:(hd + 1) * QK_ROPE].astype(BF16)
    kv_ref[0] = _mm(ckv, wukv_ref[...]).astype(BF16)
    kper_ref[0] = kpe.astype(BF16)
    qkv_ref[0] = p[:, _C_QKV:]
    ab_ref[0] = p[:, _C_AB:_C_AB + LANES]
    if emit_cache:
        ckv_ref[0] = ckv
        kpe_ref[0] = p[:, _C_KPE:_C_KPE + QK_ROPE]


def _even_in(x, shift, scale, ln, w, qg, kg, wuq, wukv, rope_tabs, per_batch_cond, emit_cache):
    b, l, d = x.shape
    tm = min(l, 256)
    cidx = (lambda i, t: (i, 0, 0)) if per_batch_cond else (lambda i, t: (0, 0, 0))
    in_specs = [pl.BlockSpec((1, tm, d), lambda i, t: (i, t, 0)),
                pl.BlockSpec((1, 1, d), cidx),
                pl.BlockSpec((1, 1, d), cidx),
                _resident((1, d)), _resident(w.shape), _resident((1, D_CQ)), _resident((1, D_CKV)),
                _resident(wuq.shape), _resident(wukv.shape)]
    args = [x, shift, scale, ln, w, qg, kg, wuq, wukv]
    if rope_tabs is not None:
        in_specs += [pl.BlockSpec((tm, LANES), lambda i, t: (t, 0))] * 2
        args += list(rope_tabs)
    kvw = A_HEADS * (QK_NOPE + V_HEAD)
    out_shape = [jax.ShapeDtypeStruct((b, A_HEADS, l, QK_NOPE + QK_ROPE), BF16),
                 jax.ShapeDtypeStruct((b, l, kvw), BF16),
                 jax.ShapeDtypeStruct((b, l, QK_ROPE), BF16),
                 jax.ShapeDtypeStruct((b, l, QKV_W), F32),
                 jax.ShapeDtypeStruct((b, l, LANES), F32)]
    out_specs = [pl.BlockSpec((1, A_HEADS, tm, QK_NOPE + QK_ROPE), lambda i, t: (i, 0, t, 0)),
                 pl.BlockSpec((1, tm, kvw), lambda i, t: (i, t, 0)),
                 pl.BlockSpec((1, tm, QK_ROPE), lambda i, t: (i, t, 0)),
                 pl.BlockSpec((1, tm, QKV_W), lambda i, t: (i, t, 0)),
                 pl.BlockSpec((1, tm, LANES), lambda i, t: (i, t, 0))]
    if emit_cache:
        out_shape += [jax.ShapeDtypeStruct((b, l, D_CKV), F32), jax.ShapeDtypeStruct((b, l, QK_ROPE), F32)]
        out_specs += [pl.BlockSpec((1, tm, D_CKV), lambda i, t: (i, t, 0)),
                      pl.BlockSpec((1, tm, QK_ROPE), lambda i, t: (i, t, 0))]
    return pl.pallas_call(
        functools.partial(_even_in_kernel, rope_tabs is not None, emit_cache),
        out_shape=out_shape, grid=(b, l // tm), in_specs=in_specs, out_specs=out_specs,
        compiler_params=_params("parallel", "parallel"),
        name="even_in",
    )(*args)


def _kv_up_kernel(c_ref, w_ref, o_ref):
    o_ref[0] = _mm(c_ref[0], w_ref[...]).astype(BF16)


def _kv_up(ckv, wukv):
    b, p, c = ckv.shape
    e = wukv.shape[1]
    return pl.pallas_call(
        _kv_up_kernel,
        out_shape=jax.ShapeDtypeStruct((b, p, e), BF16),
        grid=(b,),
        in_specs=[pl.BlockSpec((1, p, c), lambda i: (i, 0, 0)), _resident(wukv.shape)],
        out_specs=pl.BlockSpec((1, p, e), lambda i: (i, 0, 0)),
        compiler_params=_params("parallel"),
        name="kv_up",
    )(ckv, wukv)


def _attn_kernel(n_ctx, *refs):
    if n_ctx:
        q_ref, kc_ref, vc_ref, pec_ref, kl_ref, vl_ref, pel_ref, o_ref, kcat, vcat = refs
    else:
        q_ref, kl_ref, vl_ref, pel_ref, o_ref, kcat, vcat = refs

    @pl.when(pl.program_id(2) == 0)
    def _():
        if n_ctx:
            kcat[:n_ctx, :QK_NOPE] = kc_ref[0]
            kcat[:n_ctx, QK_NOPE:] = pec_ref[0].astype(BF16)
            vcat[:n_ctx, :] = vc_ref[0]
        kcat[n_ctx:, :QK_NOPE] = kl_ref[0]
        kcat[n_ctx:, QK_NOPE:] = pel_ref[0]
        vcat[n_ctx:, :] = vl_ref[0]

    scale = (QK_NOPE + QK_ROPE) ** -0.5
    s = _mm_nt(q_ref[0, 0], kcat[...]) * scale
    e = jnp.exp(s - jnp.max(s, axis=-1, keepdims=True))
    o_ref[0] = _mm(e, vcat[...]) * (1.0 / jnp.sum(e, axis=-1, keepdims=True))


def _attention(q, kv, kpe, ctx=None):
    b, hh, l, dq = q.shape
    tq = min(l, 256)
    n_ctx = 0 if ctx is None else ctx[0].shape[1]
    lk = n_ctx + l
    in_specs = [pl.BlockSpec((1, 1, tq, dq), lambda i, h, t: (i, h, t, 0))]
    args = [q]
    if ctx is not None:
        in_specs += [pl.BlockSpec((1, n_ctx, QK_NOPE), lambda i, h, t: (i, 0, 2 * h)),
                     pl.BlockSpec((1, n_ctx, V_HEAD), lambda i, h, t: (i, 0, 2 * h + 1)),
                     pl.BlockSpec((1, n_ctx, QK_ROPE), lambda i, h, t: (i, 0, 0))]
        args += [ctx[0], ctx[0], ctx[1]]
    in_specs += [pl.BlockSpec((1, l, QK_NOPE), lambda i, h, t: (i, 0, 2 * h)),
                 pl.BlockSpec((1, l, V_HEAD), lambda i, h, t: (i, 0, 2 * h + 1)),
                 pl.BlockSpec((1, l, QK_ROPE), lambda i, h, t: (i, 0, 0))]
    args += [kv, kv, kpe]
    return pl.pallas_call(
        functools.partial(_attn_kernel, n_ctx),
        out_shape=jax.ShapeDtypeStruct((b, l, hh * V_HEAD), F32),
        grid=(b, hh, l // tq),
        in_specs=in_specs,
        out_specs=pl.BlockSpec((1, tq, V_HEAD), lambda i, h, t: (i, t, h)),
        scratch_shapes=[pltpu.VMEM((lk, dq), BF16), pltpu.VMEM((lk, V_HEAD), BF16)],
        compiler_params=_params("parallel", "parallel", "arbitrary"),
        name="attention",
    )(*args)


def _solve_unit_triangular(a, r):
    y = r - _mm3(a, r)
    p = a
    for _ in range(int(math.log2(CHUNK)) - 1):
        p = _mm3(p, p)
        y = y + _mm3(p, y)
    return y


def _gdn_kernel(has_state, l, qr_ref, kr_ref, vr_ref, cwq_ref, cwk_ref, cwv_ref, abc_ref, abr_ref,
                alog_ref, dtb_ref, on_ref, *rest):
    if has_state:
        s0f_ref, s0b_ref, o_ref = rest[:3]
        scr = rest[3:]
        sf_ref = sb_ref = None
    else:
        o_ref, sf_ref, sb_ref = rest[:3]
        scr = rest[3:]
    (qs, ks, vs, u_f, w_f, qd_f, kd_f, in_f, dc_f, u_b, w_b, qd_b, kd_b, in_b, dc_b, o_f, o_b) = scr
    n = l // CHUNK
    rb = min(l, 512)

    def conv_block(i, carry):
        r0 = pl.multiple_of(i * rb, rb)
        pr0 = pl.multiple_of(jnp.maximum(r0 - 8, 0), 8)
        nr0 = pl.multiple_of(jnp.minimum(r0 + rb, l - 8), 8)
        has_prev = (r0 > 0).astype(F32)
        has_next = (r0 + rb < l).astype(F32)
        rowi = lax.broadcasted_iota(jnp.int32, (rb, DK), 0)
        for src, cw, dst, unit in ((qr_ref, cwq_ref, qs, DK ** -0.5), (kr_ref, cwk_ref, ks, 1.0),
                                   (vr_ref, cwv_ref, vs, None)):
            blk = src[0, pl.ds(r0, rb), :]
            prow = src[0, pl.ds(pr0, 8), :][7:8, :] * has_prev
            nrow = src[0, pl.ds(nr0, 8), :][0:1, :] * has_next
            down = jnp.where(rowi == 0, prow, pltpu.roll(blk, 1, 0))
            up = jnp.where(rowi == rb - 1, nrow, pltpu.roll(blk, rb - 1, 0))
            y = _silu(cw[0:1, :] * down + cw[1:2, :] * blk + cw[2:3, :] * up)
            if unit is not None:
                y = y * lax.rsqrt(jnp.sum(y * y, axis=-1, keepdims=True) + EPS) * unit
            dst[pl.ds(r0, rb), :] = y
        return carry

    lax.fori_loop(0, l // rb, conv_block, 0)

    neg_a_f = -jnp.exp(alog_ref[0, 0])
    neg_a_b = -jnp.exp(alog_ref[0, 1])
    dt_f = dtb_ref[0, 0]
    dt_b = dtb_ref[0, 1]
    rowi = lax.broadcasted_iota(jnp.int32, (CHUNK, CHUNK), 0)
    coli = lax.broadcasted_iota(jnp.int32, (CHUNK, CHUNK), 1)
    lower = coli <= rowi
    upper = coli >= rowi

    def chunk_block(c, carry):
        r0 = pl.multiple_of(c * CHUNK, CHUNK)
        q = qs[pl.ds(r0, CHUNK), :]
        k = ks[pl.ds(r0, CHUNK), :]
        v = vs[pl.ds(r0, CHUNK), :]
        abc = abc_ref[0, 0, pl.ds(r0, CHUNK), :]
        qk = _mm_nt(q, k)
        for d, (neg_a, dt, tri, tri_t, u_s, w_s, qd_s, kd_s, in_s, dc_s) in enumerate((
                (neg_a_f, dt_f, lower, upper, u_f, w_f, qd_f, kd_f, in_f, dc_f),
                (neg_a_b, dt_b, upper, lower, u_b, w_b, qd_b, kd_b, in_b, dc_b))):
            g_col = neg_a * _softplus(abc[:, d:d + 1] + dt)
            g_row = neg_a * _softplus(abr_ref[0, 0, d, pl.ds(c, 1), :] + dt)
            beta = _sigmoid(abc[:, 2 + d:3 + d])
            gc = jnp.sum(jnp.where(tri, jnp.broadcast_to(g_row, (CHUNK, CHUNK)), 0.0), axis=1, keepdims=True)
            gr = jnp.sum(jnp.where(tri_t, jnp.broadcast_to(g_col, (CHUNK, CHUNK)), 0.0), axis=0, keepdims=True)
            gamma = jnp.exp(jnp.where(tri, gc - gr, -jnp.inf))
            strict = jnp.where(rowi == coli, 0.0, gamma)
            kb = k * beta
            a = _mm_nt(kb, k) * strict
            e_gc = jnp.exp(gc)
            sol = _solve_unit_triangular(a, jnp.concatenate([v * beta, kb * e_gc], axis=1))
            g_end = gc[CHUNK - 1:CHUNK, :] if d == 0 else gc[0:1, :]
            u_s[pl.ds(r0, CHUNK), :] = sol[:, :DV]
            w_s[pl.ds(r0, CHUNK), :] = sol[:, DV:].astype(BF16)
            qd_s[pl.ds(r0, CHUNK), :] = (q * e_gc).astype(BF16)
            kd_s[pl.ds(r0, CHUNK), :] = (k * jnp.exp(g_end - gc)).astype(BF16)
            in_s[pl.ds(r0, CHUNK), :] = (qk * gamma).astype(BF16)
            dc_s[pl.ds(pl.multiple_of(c * 8, 8), 8), :] = jnp.broadcast_to(jnp.exp(g_end), (8, DV))
        return carry

    lax.fori_loop(0, n, chunk_block, 0)

    def one_step(c, s, u_s, w_s, qd_s, kd_s, in_s, dc_s, o_s):
        r0 = pl.multiple_of(c * CHUNK, CHUNK)
        sb = s.astype(BF16)
        v_new = u_s[pl.ds(r0, CHUNK), :] - _mm(w_s[pl.ds(r0, CHUNK), :], sb)
        vb = v_new.astype(BF16)
        o_s[pl.ds(r0, CHUNK), :] = _mm(qd_s[pl.ds(r0, CHUNK), :], sb) + _mm(in_s[pl.ds(r0, CHUNK), :], vb)
        dec = dc_s[pl.ds(pl.multiple_of(c * 8, 8), 8), :][0:1, :]
        return s * dec + _mm_tn(kd_s[pl.ds(r0, CHUNK), :], vb)

    def scan_step(c, carry):
        sf, sb = carry
        sf = one_step(c, sf, u_f, w_f, qd_f, kd_f, in_f, dc_f, o_f)
        sb = one_step(n - 1 - c, sb, u_b, w_b, qd_b, kd_b, in_b, dc_b, o_b)
        return sf, sb

    if has_state:
        init = (s0f_ref[0, 0], s0b_ref[0, 0])
    else:
        init = (jnp.zeros((DK, DV), F32), jnp.zeros((DK, DV), F32))
    sf, sb = lax.fori_loop(0, n, scan_step, init)
    if not has_state:
        sf_ref[0, 0] = sf
        sb_ref[0, 0] = sb

    def out_block(i, carry):
        r0 = pl.multiple_of(i * rb, rb)
        o_ref[0, pl.ds(r0, rb), :] = _rms(o_f[pl.ds(r0, rb), :] + o_b[pl.ds(r0, rb), :], on_ref[...])
        return carry

    lax.fori_loop(0, l // rb, out_block, 0)


def _gdn(qkv, ab, conv, a_log, dt_bias, o_norm, states):
    b, l, _ = qkv.shape
    n = l // CHUNK
    hh = B_HEADS
    ab4 = jnp.transpose(ab[:, :, :AB_W].reshape(b, l, 4, hh), (0, 3, 1, 2))
    abr = jnp.transpose(ab[:, :, :AB_W].reshape(b, n, CHUNK, 4, hh), (0, 4, 3, 1, 2))
    alog = jnp.transpose(a_log).reshape(hh, 2, 1, 1)
    dtb = jnp.transpose(dt_bias).reshape(hh, 2, 1, 1)
    has_state = states is not None
    in_specs = [pl.BlockSpec((1, l, DK), lambda i, h: (i, 0, h)),
                pl.BlockSpec((1, l, DK), lambda i, h: (i, 0, hh + h)),
                pl.BlockSpec((1, l, DV), lambda i, h: (i, 0, 2 * hh + h)),
                pl.BlockSpec((3, DK), lambda i, h: (0, h)),
                pl.BlockSpec((3, DK), lambda i, h: (0, hh + h)),
                pl.BlockSpec((3, DV), lambda i, h: (0, 2 * hh + h)),
                pl.BlockSpec((1, 1, l, 4), lambda i, h: (i, h, 0, 0)),
                pl.BlockSpec((1, 1, 4, n, CHUNK), lambda i, h: (i, h, 0, 0, 0)),
                pl.BlockSpec((1, 2, 1, 1), lambda i, h: (h, 0, 0, 0)),
                pl.BlockSpec((1, 2, 1, 1), lambda i, h: (h, 0, 0, 0)),
                pl.BlockSpec((1, DV), lambda i, h: (0, 0))]
    args = [qkv, qkv, qkv, conv, conv, conv, ab4, abr, alog, dtb, o_norm.reshape(1, DV)]
    st_spec = pl.BlockSpec((1, 1, DK, DV), lambda i, h: (i, h, 0, 0))
    out_shape = [jax.ShapeDtypeStruct((b, l, hh * DV), F32)]
    out_specs = [pl.BlockSpec((1, l, DV), lambda i, h: (i, 0, h))]
    if has_state:
        in_specs += [st_spec, st_spec]
        args += list(states)
    else:
        out_shape += [jax.ShapeDtypeStruct((b, hh, DK, DV), F32)] * 2
        out_specs += [st_spec, st_spec]
    per_dir = [pltpu.VMEM((l, DV), F32), pltpu.VMEM((l, DK), BF16), pltpu.VMEM((l, DK), BF16),
               pltpu.VMEM((l, DK), BF16), pltpu.VMEM((l, CHUNK), BF16), pltpu.VMEM((n * 8, DV), F32)]
    scratch = ([pltpu.VMEM((l, DK), F32), pltpu.VMEM((l, DK), F32), pltpu.VMEM((l, DV), F32)]
               + per_dir + per_dir + [pltpu.VMEM((l, DV), F32), pltpu.VMEM((l, DV), F32)])
    return pl.pallas_call(
        functools.partial(_gdn_kernel, has_state, l),
        out_shape=out_shape, grid=(b, hh), in_specs=in_specs, out_specs=out_specs,
        scratch_shapes=scratch,
        compiler_params=_params("parallel", "parallel"),
        name="gated_deltanet",
    )(*args)


def _even_out_kernel(x_ref, sh_ref, sc_ref, gt_ref, ln_ref, at_ref, gd_ref, wz_ref, wo_ref, o_ref):
    x = x_ref[0]
    z = _mm(_adaln(x, ln_ref[...], sc_ref[0], sh_ref[0]), wz_ref[...])
    y = jnp.concatenate([at_ref[0], gd_ref[0]], axis=1) * _silu(z)
    o_ref[0] = x + gt_ref[0] * _mm(y, wo_ref[...])


def _even_out(x, shift, scale, gate, ln, attn, gdn, wz, wo, per_batch_cond):
    b, l, d = x.shape
    tm = min(l, 256)
    cidx = (lambda i, t: (i, 0, 0)) if per_batch_cond else (lambda i, t: (0, 0, 0))
    row = lambda i, t: (i, t, 0)
    return pl.pallas_call(
        _even_out_kernel,
        out_shape=jax.ShapeDtypeStruct((b, l, d), F32),
        grid=(b, l // tm),
        in_specs=[pl.BlockSpec((1, tm, d), row), pl.BlockSpec((1, 1, d), cidx), pl.BlockSpec((1, 1, d), cidx),
                  pl.BlockSpec((1, 1, d), cidx), _resident((1, d)),
                  pl.BlockSpec((1, tm, attn.shape[2]), row), pl.BlockSpec((1, tm, gdn.shape[2]), row),
                  _resident(wz.shape), _resident(wo.shape)],
        out_specs=pl.BlockSpec((1, tm, d), row),
        compiler_params=_params("parallel", "parallel"),
        name="even_out",
    )(x, shift, scale, gate, ln, attn, gdn, wz, wo)


def _pool_kernel(final, l, tm, xp_ref, x_ref, xn_ref, sh_ref, sc_ref, gt_ref, ln_ref, wp_ref, wz_ref,
                 wg_ref, ps_ref, wo_ref, fn_ref, o_ref):
    t = pl.program_id(1)
    x = x_ref[0]
    rows = tm + 2 * POOL_HALO
    xa = jnp.concatenate([xp_ref[0], x, xn_ref[0]], axis=0)
    h = _adaln(xa, ln_ref[...], sc_ref[0], sh_ref[0])
    ri = lax.broadcasted_iota(jnp.int32, (rows, 1), 0) + (t * tm - POOL_HALO)
    inside = jnp.logical_and(ri >= 0, ri < l)
    pin = jnp.where(inside, _mm(h, wp_ref[...]), 0.0)
    z = _mm(h[POOL_HALO:POOL_HALO + tm], wz_ref[...])
    pos = lax.broadcasted_iota(jnp.int32, (tm, 1), 0) + t * tm
    gw = pin.shape[1] // len(POOL_WINDOWS)
    mixed = []
    for gi, w in enumerate(POOL_WINDOWS):
        pg = pin[:, gi * gw:(gi + 1) * gw]
        acc = pg
        span = 1
        while span < w:
            acc = acc + pltpu.roll(acc, rows - span, 0)
            span *= 2
        first = POOL_HALO - w // 2
        win = pltpu.roll(acc, rows - first, 0)[:tm] if first else acc[:tm]
        cnt = (jnp.minimum(pos + (w - w // 2), l) - jnp.maximum(pos - w // 2, 0)).astype(F32)
        pooled = win / cnt - pg[POOL_HALO:POOL_HALO + tm]
        mixed.append(_mm(pooled, wg_ref[gi]))
    y = jnp.concatenate(mixed, axis=1) * ps_ref[...] * _silu(z)
    out = x + gt_ref[0] * _mm(y, wo_ref[...])
    o_ref[0] = _rms(out, fn_ref[...]) if final else out


def _pool_layer(x, shift, scale, gate, ln, wp, wz, wg, ps, wo, fnorm, per_batch_cond, final):
    b, l, d = x.shape
    tm = min(l, 256)
    hb = tm // POOL_HALO
    nb = l // POOL_HALO
    cidx = (lambda i, t: (i, 0, 0)) if per_batch_cond else (lambda i, t: (0, 0, 0))
    row = lambda i, t: (i, t, 0)
    return pl.pallas_call(
        functools.partial(_pool_kernel, final, l, tm),
        out_shape=jax.ShapeDtypeStruct((b, l, d), F32),
        grid=(b, l // tm),
        in_specs=[pl.BlockSpec((1, POOL_HALO, d), lambda i, t: (i, jnp.maximum(t * hb - 1, 0), 0)),
                  pl.BlockSpec((1, tm, d), row),
                  pl.BlockSpec((1, POOL_HALO, d), lambda i, t: (i, jnp.minimum((t + 1) * hb, nb - 1), 0)),
                  pl.BlockSpec((1, 1, d), cidx), pl.BlockSpec((1, 1, d), cidx), pl.BlockSpec((1, 1, d), cidx),
                  _resident((1, d)), _resident(wp.shape), _resident(wz.shape), _resident(wg.shape),
                  _resident((1, d)), _resident(wo.shape), _resident((1, d))],
        out_specs=pl.BlockSpec((1, tm, d), row),
        compiler_params=_params("parallel", "parallel"),
        name="pool_mixer",
    )(x, x, x, shift, scale, gate, ln, wp, wz, wg, ps, wo, fnorm)


def _rope_tables(n_tokens):
    rows = n_tokens // GRID_W
    row_pos = jnp.repeat(jnp.arange(rows), GRID_W).astype(F32)
    col_pos = jnp.tile(jnp.arange(GRID_W), rows).astype(F32)
    half = QK_ROPE // 2
    inv_freq = ROPE_BASE ** (-jnp.arange(0, half, 2, dtype=F32) / half)
    ar = row_pos[:, None] * inv_freq
    ac = col_pos[:, None] * inv_freq
    cos = jnp.concatenate([jnp.cos(ar), jnp.cos(ar), jnp.cos(ac), jnp.cos(ac)], axis=1)
    sin = jnp.concatenate([-jnp.sin(ar), jnp.sin(ar), -jnp.sin(ac), jnp.sin(ac)], axis=1)
    return jnp.concatenate([cos, cos], axis=1), jnp.concatenate([sin, sin], axis=1)


def _swap_perm():
    q = QK_ROPE // 4
    return jnp.array(list(range(q, 2 * q)) + list(range(q)) + list(range(3 * q, 4 * q)) + list(range(2 * q, 3 * q)))


def _even_weights(w_in, w_uq, w_ukv, w_out):
    d = w_in.shape[0]
    perm = _swap_perm()
    o = 0
    cq = w_in[:, o:o + D_CQ]; o += D_CQ
    ckv = w_in[:, o:o + D_CKV]; o += D_CKV
    kpe = w_in[:, o:o + QK_ROPE]; o += QK_ROPE
    qkv = w_in[:, o:o + QKV_W]; o += QKV_W
    ab = w_in[:, o:o + AB_W]; o += AB_W
    wz = w_in[:, o:]
    pad = lambda n: jnp.zeros((d, n), w_in.dtype)
    w = jnp.concatenate([cq, ckv, kpe, pad(LANES - QK_ROPE), kpe[:, perm], pad(LANES - QK_ROPE),
                         ab, pad(LANES - AB_W), qkv], axis=1).astype(BF16)
    uq = w_uq.reshape(D_CQ, A_HEADS, QK_NOPE + QK_ROPE)
    pe = uq[:, :, QK_NOPE:]
    wuq = jnp.concatenate([uq[:, :, :QK_NOPE].reshape(D_CQ, -1), pe.reshape(D_CQ, -1),
                           pe[:, :, perm].reshape(D_CQ, -1)], axis=1).astype(BF16)
    return w, wuq, w_ukv.astype(BF16), wz.astype(BF16), w_out.astype(BF16)


def kernel(x_prompt, x_sample, cache_ckv, cache_kpe, state_fwd, state_bwd, c, c_ctx, ln_e, mod_w_e, mod_b_e, w_in_e, q_norm_e, kv_norm_e, w_uq_e, w_ukv_e, conv_e, a_log_e, dt_bias_e, o_norm_e, w_out_e, ln_o, mod_w_o, mod_b_o, w_in_o, w_pool_o, pool_scale_o, w_out_o, final_norm):
    d = x_prompt.shape[-1]
    nb = c.shape[0]
    depth = ln_e.shape[0] + ln_o.shape[0]
    assert depth % 2 == 0, "the final norm is fused into the last (pooling) layer"
    cond = jnp.concatenate([c_ctx[None, :], c, jnp.zeros((COND_ROWS - 1 - nb, d), F32)], axis=0)
    mod = {0: _modulation(cond, mod_w_e, mod_b_e), 1: _modulation(cond, mod_w_o, mod_b_o)}
    rope = _rope_tables(x_sample.shape[1])
    fnorm = final_norm.reshape(1, d)
    xp, xs = x_prompt, x_sample
    ckv_out, kpe_out, sf_out, sb_out = [], [], [], []
    for layer in range(depth):
        i = layer // 2
        m = mod[layer % 2][i]
        sh_p, sc_p, gt_p = (m[0:1, j * d:(j + 1) * d].reshape(1, 1, d) for j in range(3))
        sh_s, sc_s, gt_s = (m[1:1 + nb, j * d:(j + 1) * d].reshape(nb, 1, d) for j in range(3))
        if layer % 2 == 0:
            ln = ln_e[i].reshape(1, d)
            w, wuq, wukv, wz, wo = _even_weights(w_in_e[i], w_uq_e[i], w_ukv_e[i], w_out_e[i])
            qg, kg = q_norm_e[i].reshape(1, D_CQ), kv_norm_e[i].reshape(1, D_CKV)
            gdn_w = (conv_e[i], a_log_e[i], dt_bias_e[i], o_norm_e[i])
            q, kv, kper, qkv, ab, ckv, kpe = _even_in(xp, sh_p, sc_p, ln, w, qg, kg, wuq, wukv, None, False, True)
            attn = _attention(q, kv, kper)
            g_o, s_f, s_b = _gdn(qkv, ab, *gdn_w, None)
            xp = _even_out(xp, sh_p, sc_p, gt_p, ln, attn, g_o, wz, wo, False)
            ckv_out.append(ckv); kpe_out.append(kpe); sf_out.append(s_f); sb_out.append(s_b)
            q, kv, kper, qkv, ab = _even_in(xs, sh_s, sc_s, ln, w, qg, kg, wuq, wukv, rope, True, False)
            attn = _attention(q, kv, kper, (_kv_up(cache_ckv[:, i], wukv), cache_kpe[:, i]))
            (g_o,) = _gdn(qkv, ab, *gdn_w, (state_fwd[:, i], state_bwd[:, i]))
            xs = _even_out(xs, sh_s, sc_s, gt_s, ln, attn, g_o, wz, wo, True)
        else:
            ln = ln_o[i].reshape(1, d)
            wp = w_in_o[i][:, :d].astype(BF16)
            wzo = w_in_o[i][:, d:].astype(BF16)
            wg = w_pool_o[i].astype(BF16)
            ps = pool_scale_o[i].reshape(1, d)
            wo = w_out_o[i].astype(BF16)
            final = layer == depth - 1
            xp = _pool_layer(xp, sh_p, sc_p, gt_p, ln, wp, wzo, wg, ps, wo, fnorm, False, final)
            xs = _pool_layer(xs, sh_s, sc_s, gt_s, ln, wp, wzo, wg, ps, wo, fnorm, True, final)
    return (xp, xs, jnp.stack(ckv_out, axis=1), jnp.stack(kpe_out, axis=1),
            jnp.stack(sf_out, axis=1), jnp.stack(sb_out, axis=1))
```

```python
import functools
import math

import jax
import jax.numpy as jnp
from jax import lax
from jax.experimental import pallas as pl
from jax.experimental.pallas import tpu as pltpu

F32 = jnp.float32
BF16 = jnp.bfloat16

EPS = 1e-6
GRID_W = 64
ROPE_BASE = 10000.0
A_HEADS = 8
QK_NOPE = 128
QK_ROPE = 64
V_HEAD = 128
D_CQ = 512
D_CKV = 512
B_HEADS = 8
DK = 128
DV = 128
CHUNK = 64
INV_BLOCK = 16
GDN_GROUP = 16
ATTN_BLOCK = 256
ATTN_TILE = 2048
POOL_WINDOWS = (2, 4, 8, 16)
POOL_HALO = 8
QKV_W = B_HEADS * (2 * DK + DV)
AB_W = 4 * B_HEADS
LANES = 128
V7X_VMEM_BYTES = 64 * 1024 * 1024
VMEM_LIMIT = V7X_VMEM_BYTES - 8 * 1024 * 1024
COND_ROWS = 16

_C_CQ = 0
_C_CKV = D_CQ
_C_KPE = D_CQ + D_CKV
_C_KPE_SW = _C_KPE + LANES
_C_AB = _C_KPE_SW + LANES
_C_QKV = _C_AB + LANES
IN_W = _C_QKV + QKV_W


def _mm(a, b):
    return jnp.dot(a.astype(BF16), b.astype(BF16), preferred_element_type=F32)


def _mm_nt(a, b):
    return lax.dot_general(a.astype(BF16), b.astype(BF16), (((1,), (1,)), ((), ())),
                           preferred_element_type=F32)


def _mm_tn(a, b):
    return lax.dot_general(a.astype(BF16), b.astype(BF16), (((0,), (0,)), ((), ())),
                           preferred_element_type=F32)


def _sigmoid(x):
    return 1.0 / (1.0 + jnp.exp(-x))


def _silu(x):
    return x * _sigmoid(x)


def _softplus(x):
    return jnp.maximum(x, 0.0) + jnp.log(1.0 + jnp.exp(-jnp.abs(x)))


def _rms(x, g):
    return x * lax.rsqrt(jnp.mean(x * x, axis=-1, keepdims=True) + EPS) * g


def _adaln(x, ln, sc, sh):
    return _rms(x, ln) * (1.0 + sc) + sh


def _params(*sem):
    return pltpu.CompilerParams(dimension_semantics=sem, vmem_limit_bytes=VMEM_LIMIT)


def _resident(shape):
    nd = len(shape)
    return pl.BlockSpec(shape, lambda *_: (0,) * nd, pipeline_mode=pl.Buffered(1))


def _mod_kernel(c_ref, w_ref, b_ref, o_ref):
    o_ref[0] = _mm(_silu(c_ref[...]), w_ref[0]) + b_ref[0]


def _modulation(cond, w, b):
    n, d, e = w.shape
    tn = 1024
    return pl.pallas_call(
        _mod_kernel,
        out_shape=jax.ShapeDtypeStruct((n, COND_ROWS, e), F32),
        grid=(n, e // tn),
        in_specs=[pl.BlockSpec((COND_ROWS, d), lambda l, j: (0, 0)),
                  pl.BlockSpec((1, d, tn), lambda l, j: (l, 0, j)),
                  pl.BlockSpec((1, 1, tn), lambda l, j: (l, 0, j))],
        out_specs=pl.BlockSpec((1, COND_ROWS, tn), lambda l, j: (l, 0, j)),
        compiler_params=_params("parallel", "parallel"),
        name="modulation",
    )(cond, w, b.reshape(n, 1, e))


def _even_in_kernel(rope, emit_cache, x_ref, sh_ref, sc_ref, ln_ref, w_ref, qg_ref, kg_ref, wuq_ref, wukv_ref, *rest):
    if rope:
        cos_ref, sin_ref, *outs = rest
    else:
        outs = rest
    if emit_cache:
        q_ref, kv_ref, kper_ref, qkv_ref, ab_ref, ckv_ref, kpe_ref = outs
    else:
        q_ref, kv_ref, kper_ref, qkv_ref, ab_ref = outs
    h = _adaln(x_ref[0], ln_ref[...], sc_ref[0], sh_ref[0])
    p = _mm(h, w_ref[...])
    kpe = p[:, _C_KPE:_C_KPE + QK_ROPE]
    ckv = _rms(p[:, _C_CKV:_C_CKV + D_CKV], kg_ref[...])
    qq = _mm(_rms(p[:, _C_CQ:_C_CQ + D_CQ], qg_ref[...]), wuq_ref[...])
    n_nope = A_HEADS * QK_NOPE
    n_pe = A_HEADS * QK_ROPE
    qpe = qq[:, n_nope:n_nope + n_pe]
    if rope:
        cos = cos_ref[...]
        sin = sin_ref[...]
        kpe = kpe * cos[:, :QK_ROPE] + p[:, _C_KPE_SW:_C_KPE_SW + QK_ROPE] * sin[:, :QK_ROPE]
        reps = n_pe // LANES
        qpe = (qpe * jnp.concatenate([cos] * reps, axis=1)
               + qq[:, n_nope + n_pe:] * jnp.concatenate([sin] * reps, axis=1))
    for hd in range(A_HEADS):
        q_ref[0, hd, :, :QK_NOPE] = qq[:, hd * QK_NOPE:(hd + 1) * QK_NOPE].astype(BF16)
        q_ref[0, hd, :, QK_NOPE:] = qpe[:, hd * QK_ROPE:(hd + 1) * QK_ROPE].astype(BF16)
    kv_ref[0] = _mm(ckv, wukv_ref[...]).astype(BF16)
    kper_ref[0] = kpe.astype(BF16)
    qkv_ref[0] = p[:, _C_QKV:]
    ab_ref[0] = p[:, _C_AB:_C_AB + LANES]
    if emit_cache:
        ckv_ref[0] = ckv
        kpe_ref[0] = p[:, _C_KPE:_C_KPE + QK_ROPE]


def _even_in(x, shift, scale, ln, w, qg, kg, wuq, wukv, rope_tabs, per_batch_cond, emit_cache):
    b, l, d = x.shape
    tm = min(l, 256)
    cidx = (lambda i, t: (i, 0, 0)) if per_batch_cond else (lambda i, t: (0, 0, 0))
    in_specs = [pl.BlockSpec((1, tm, d), lambda i, t: (i, t, 0)),
                pl.BlockSpec((1, 1, d), cidx),
                pl.BlockSpec((1, 1, d), cidx),
                _resident((1, d)), _resident(w.shape), _resident((1, D_CQ)), _resident((1, D_CKV)),
                _resident(wuq.shape), _resident(wukv.shape)]
    args = [x, shift, scale, ln, w, qg, kg, wuq, wukv]
    if rope_tabs is not None:
        in_specs += [pl.BlockSpec((tm, LANES), lambda i, t: (t, 0))] * 2
        args += list(rope_tabs)
    kvw = A_HEADS * (QK_NOPE + V_HEAD)
    out_shape = [jax.ShapeDtypeStruct((b, A_HEADS, l, QK_NOPE + QK_ROPE), BF16),
                 jax.ShapeDtypeStruct((b, l, kvw), BF16),
                 jax.ShapeDtypeStruct((b, l, QK_ROPE), BF16),
                 jax.ShapeDtypeStruct((b, l, QKV_W), F32),
                 jax.ShapeDtypeStruct((b, l, LANES), F32)]
    out_specs = [pl.BlockSpec((1, A_HEADS, tm, QK_NOPE + QK_ROPE), lambda i, t: (i, 0, t, 0)),
                 pl.BlockSpec((1, tm, kvw), lambda i, t: (i, t, 0)),
                 pl.BlockSpec((1, tm, QK_ROPE), lambda i, t: (i, t, 0)),
                 pl.BlockSpec((1, tm, QKV_W), lambda i, t: (i, t, 0)),
                 pl.BlockSpec((1, tm, LANES), lambda i, t: (i, t, 0))]
    if emit_cache:
        out_shape += [jax.ShapeDtypeStruct((b, l, D_CKV), F32), jax.ShapeDtypeStruct((b, l, QK_ROPE), F32)]
        out_specs += [pl.BlockSpec((1, tm, D_CKV), lambda i, t: (i, t, 0)),
                      pl.BlockSpec((1, tm, QK_ROPE), lambda i, t: (i, t, 0))]
    return pl.pallas_call(
        functools.partial(_even_in_kernel, rope_tabs is not None, emit_cache),
        out_shape=out_shape, grid=(b, l // tm), in_specs=in_specs, out_specs=out_specs,
        compiler_params=_params("parallel", "parallel"),
        name="even_in",
    )(*args)


def _kv_up_kernel(c_ref, w_ref, o_ref):
    o_ref[0] = _mm(c_ref[0], w_ref[...]).astype(BF16)


def _kv_up(ckv, wukv):
    b, p, c = ckv.shape
    e = wukv.shape[1]
    return pl.pallas_call(
        _kv_up_kernel,
        out_shape=jax.ShapeDtypeStruct((b, p, e), BF16),
        grid=(b,),
        in_specs=[pl.BlockSpec((1, p, c), lambda i: (i, 0, 0)), _resident(wukv.shape)],
        out_specs=pl.BlockSpec((1, p, e), lambda i: (i, 0, 0)),
        compiler_params=_params("parallel"),
        name="kv_up",
    )(ckv, wukv)


def _attn_kernel(n_ctx, hb, th, *refs):
    if n_ctx:
        q_ref, kvc_ref, pec_ref, kvl_ref, pel_ref, o_ref, kcat, vcat = refs
    else:
        q_ref, kvl_ref, pel_ref, o_ref, kcat, vcat = refs
    hw = QK_NOPE + V_HEAD

    @pl.when(pl.program_id(2) == 0)
    def _():
        for hd in range(hb):
            if n_ctx:
                kcat[hd, :n_ctx, :QK_NOPE] = kvc_ref[0, :, hd * hw:hd * hw + QK_NOPE]
                kcat[hd, :n_ctx, QK_NOPE:] = pec_ref[0].astype(BF16)
                vcat[hd, :n_ctx, :] = kvc_ref[0, :, hd * hw + QK_NOPE:(hd + 1) * hw]
            kcat[hd, n_ctx:, :QK_NOPE] = kvl_ref[0, :, hd * hw:hd * hw + QK_NOPE]
            kcat[hd, n_ctx:, QK_NOPE:] = pel_ref[0]
            vcat[hd, n_ctx:, :] = kvl_ref[0, :, hd * hw + QK_NOPE:(hd + 1) * hw]

    c = (QK_NOPE + QK_ROPE) ** -0.5 * math.log2(math.e)
    blocks = [(hd, j) for hd in range(hb) for j in range(q_ref.shape[2] // th)]

    def scores(hd, j):
        return _mm_nt(q_ref[0, hd, j * th:(j + 1) * th, :], kcat[hd])

    ahead = 2
    pending = [scores(*blk) for blk in blocks[:ahead]]
    for i, (hd, j) in enumerate(blocks):
        s = pending.pop(0)
        e = jnp.exp2((s - jnp.max(s, axis=-1, keepdims=True)) * c)
        o = _mm(e, vcat[hd])
        if i + ahead < len(blocks):
            pending.append(scores(*blocks[i + ahead]))
        o_ref[0, j * th:(j + 1) * th, hd * V_HEAD:(hd + 1) * V_HEAD] = o * (1.0 / jnp.sum(e, axis=-1, keepdims=True))


def _attention(q, kv, kpe, ctx=None):
    b, hh, l, dq = q.shape
    th = min(l, ATTN_BLOCK)
    tq = min(l, ATTN_TILE)
    hb = min(hh, ATTN_TILE // tq)
    hw = QK_NOPE + V_HEAD
    n_ctx = 0 if ctx is None else ctx[0].shape[1]
    lk = n_ctx + l
    in_specs = [pl.BlockSpec((1, hb, tq, dq), lambda i, h, t: (i, h, t, 0))]
    args = [q]
    if ctx is not None:
        in_specs += [pl.BlockSpec((1, n_ctx, hb * hw), lambda i, h, t: (i, 0, h)),
                     pl.BlockSpec((1, n_ctx, QK_ROPE), lambda i, h, t: (i, 0, 0))]
        args += list(ctx)
    in_specs += [pl.BlockSpec((1, l, hb * hw), lambda i, h, t: (i, 0, h)),
                 pl.BlockSpec((1, l, QK_ROPE), lambda i, h, t: (i, 0, 0))]
    args += [kv, kpe]
    return pl.pallas_call(
        functools.partial(_attn_kernel, n_ctx, hb, th),
        out_shape=jax.ShapeDtypeStruct((b, l, hh * V_HEAD), F32),
        grid=(b, hh // hb, l // tq),
        in_specs=in_specs,
        out_specs=pl.BlockSpec((1, tq, hb * V_HEAD), lambda i, h, t: (i, t, h)),
        scratch_shapes=[pltpu.VMEM((hb, lk, dq), BF16), pltpu.VMEM((hb, lk, V_HEAD), BF16)],
        compiler_params=_params("parallel", "parallel", "arbitrary"),
        name="attention",
    )(*args)


def _unit_triangular_inverses(mats, rowi, coli):
    eye = (rowi == coli).astype(F32)
    same = (rowi // INV_BLOCK) == (coli // INV_BLOCK)
    ps = [jnp.where(same, a, 0.0) for a in mats]
    ts = [eye - p for p in ps]
    for _ in range(int(math.log2(INV_BLOCK)) - 1):
        ps = [_mm(p, p) for p in ps]
        ts = [t + _mm(t, p) for t, p in zip(ts, ps)]
    size = INV_BLOCK
    while size < CHUNK:
        wider = (rowi // (2 * size)) == (coli // (2 * size))
        pick = jnp.logical_and(wider, jnp.logical_not(same))
        cts = [_mm(jnp.where(pick, a, 0.0), t) for a, t in zip(mats, ts)]
        ts = [t - _mm(t, ct) for t, ct in zip(ts, cts)]
        same = wider
        size *= 2
    return ts


def _gdn_kernel(has_state, l, hb, group, qr_ref, kr_ref, vr_ref, cwq_ref, cwk_ref, cwv_ref, abc_ref, abr_ref,
                alog_ref, dtb_ref, on_ref, *rest):
    if has_state:
        s0f_ref, s0b_ref, o_ref = rest[:3]
        scr = rest[3:]
        sf_ref = sb_ref = None
    else:
        o_ref, sf_ref, sb_ref = rest[:3]
        scr = rest[3:]
    (qs, ks, vs, km_f, nm_f, qp_f, o_f, dc_f, km_b, nm_b, qp_b, o_b, dc_b) = scr
    per_dir = ((km_f, nm_f, qp_f, o_f, dc_f), (km_b, nm_b, qp_b, o_b, dc_b))
    n = l // CHUNK
    rb = min(l, 512)
    heads = range(hb)

    def conv_block(i, carry):
        r0 = pl.multiple_of(i * rb, rb)
        pr0 = pl.multiple_of(jnp.maximum(r0 - 8, 0), 8)
        nr0 = pl.multiple_of(jnp.minimum(r0 + rb, l - 8), 8)
        has_prev = (r0 > 0).astype(F32)
        has_next = (r0 + rb < l).astype(F32)
        rowi = lax.broadcasted_iota(jnp.int32, (rb, DK), 0)
        for hd in heads:
            lanes = slice(hd * DK, (hd + 1) * DK)
            for src, cw, dst, unit in ((qr_ref, cwq_ref, qs, DK ** -0.5), (kr_ref, cwk_ref, ks, 1.0),
                                       (vr_ref, cwv_ref, vs, None)):
                blk = src[0, pl.ds(r0, rb), lanes]
                prow = src[0, pl.ds(pr0, 8), lanes][7:8, :] * has_prev
                nrow = src[0, pl.ds(nr0, 8), lanes][0:1, :] * has_next
                down = jnp.where(rowi == 0, prow, pltpu.roll(blk, 1, 0))
                up = jnp.where(rowi == rb - 1, nrow, pltpu.roll(blk, rb - 1, 0))
                y = _silu(cw[0:1, lanes] * down + cw[1:2, lanes] * blk + cw[2:3, lanes] * up)
                if unit is not None:
                    y = y * lax.rsqrt(jnp.sum(y * y, axis=-1, keepdims=True) + EPS) * unit
                dst[hd, pl.ds(r0, rb), :] = y
        return carry

    lax.fori_loop(0, l // rb, conv_block, 0)

    neg_a = [[-jnp.exp(alog_ref[hd, d]) for d in (0, 1)] for hd in heads]
    dt = [[dtb_ref[hd, d] for d in (0, 1)] for hd in heads]
    rowi = lax.broadcasted_iota(jnp.int32, (CHUNK, CHUNK), 0)
    coli = lax.broadcasted_iota(jnp.int32, (CHUNK, CHUNK), 1)
    lower = coli <= rowi
    upper = coli >= rowi

    def load_chunk(hd, c):
        r0 = pl.multiple_of(c * CHUNK, CHUNK)
        return (qs[hd, pl.ds(r0, CHUNK), :], ks[hd, pl.ds(r0, CHUNK), :], vs[hd, pl.ds(r0, CHUNK), :],
                abc_ref[0, hd, pl.ds(r0, CHUNK), :],
                [abr_ref[0, hd, d, pl.ds(c, 1), :] for d in (0, 1)])

    def gates(hd, d, k, v, abc, abr):
        tri, tri_t = (lower, upper) if d == 0 else (upper, lower)
        g_col = neg_a[hd][d] * _softplus(abc[:, d:d + 1] + dt[hd][d])
        g_row = neg_a[hd][d] * _softplus(abr + dt[hd][d])
        beta = _sigmoid(abc[:, 2 + d:3 + d])
        gc = jnp.sum(jnp.where(tri, jnp.broadcast_to(g_row, (CHUNK, CHUNK)), 0.0), axis=1, keepdims=True)
        gr = jnp.sum(jnp.where(tri_t, jnp.broadcast_to(g_col, (CHUNK, CHUNK)), 0.0), axis=0, keepdims=True)
        gamma = jnp.exp(jnp.where(tri, gc - gr, -jnp.inf))
        kb = k * beta
        e_gc = jnp.exp(gc)
        g_end = gc[CHUNK - 1:CHUNK, :] if d == 0 else gc[0:1, :]
        return dict(kb=kb, gamma=gamma, strict=jnp.where(rowi == coli, 0.0, gamma), e_gc=e_gc,
                    rhs=jnp.concatenate([v * beta, kb * e_gc], axis=1),
                    kd=(k * jnp.exp(g_end - gc)).astype(BF16), dec=jnp.broadcast_to(jnp.exp(g_end), (8, DV)))

    def chunk_group(i, carry):
        where = [(hd, i * group + j) for hd in heads for j in range(group)]
        loaded = [load_chunk(hd, c) for hd, c in where]
        both = [[gates(hd, d, k, v, abc, abr[d]) for d in (0, 1)]
                for (hd, _), (q, k, v, abc, abr) in zip(where, loaded)]
        stacked = [_mm_nt(jnp.concatenate([q, g[0]["kb"], g[1]["kb"]], axis=0), k)
                   for (q, k, *_), g in zip(loaded, both)]
        items = [dict(g[d], q=ld[0], qk=st[:CHUNK], kk=st[(1 + d) * CHUNK:(2 + d) * CHUNK], pos=pos, d=d)
                 for pos, ld, g, st in zip(where, loaded, both, stacked) for d in (0, 1)]
        invs = _unit_triangular_inverses([it["kk"] * it["strict"] for it in items], rowi, coli)
        sols = [_mm(t, it["rhs"]).astype(BF16) for t, it in zip(invs, items)]
        intras = [(it["qk"] * it["gamma"]).astype(BF16) for it in items]
        kd_sol = [_mm_tn(it["kd"], s) for it, s in zip(items, sols)]
        in_sol = [_mm(x, s) for x, s in zip(intras, sols)]
        for it, ks_, is_ in zip(items, kd_sol, in_sol):
            hd, c = it["pos"]
            km_s, nm_s, qp_s, o_s, dc_s = per_dir[it["d"]]
            r0 = pl.multiple_of(c * CHUNK, CHUNK)
            m0 = pl.multiple_of(c * DK, DK)
            nm_s[hd, pl.ds(m0, DK), :] = ks_[:, :DV]
            km_s[hd, pl.ds(m0, DK), :] = ks_[:, DV:].astype(BF16)
            o_s[hd, pl.ds(r0, CHUNK), :] = is_[:, :DV]
            qp_s[hd, pl.ds(r0, CHUNK), :] = (it["q"] * it["e_gc"] - is_[:, DV:]).astype(BF16)
            dc_s[hd, pl.ds(pl.multiple_of(c * 8, 8), 8), :] = it["dec"]
        return carry

    lax.fori_loop(0, n // group, chunk_group, 0)

    def load_step(hd, d, c):
        km_s, nm_s, qp_s, o_s, dc_s = per_dir[d]
        r0 = pl.multiple_of(c * CHUNK, CHUNK)
        m0 = pl.multiple_of(c * DK, DK)
        return (km_s[hd, pl.ds(m0, DK), :], nm_s[hd, pl.ds(m0, DK), :], qp_s[hd, pl.ds(r0, CHUNK), :],
                o_s[hd, pl.ds(r0, CHUNK), :], dc_s[hd, pl.ds(pl.multiple_of(c * 8, 8), 8), :][0:1, :])

    chains = [(hd, d) for hd in heads for d in (0, 1)]

    def scan_step(c, states):
        cs = [c if d == 0 else n - 1 - c for _, d in chains]
        loaded = [load_step(hd, d, cc) for (hd, d), cc in zip(chains, cs)]
        sbs = [s.astype(BF16) for s in states]
        new = [s * dec + nm - _mm(km, sb) for (km, nm, qp, o_loc, dec), s, sb in zip(loaded, states, sbs)]
        outs = [o_loc + _mm(qp, sb) for (km, nm, qp, o_loc, dec), sb in zip(loaded, sbs)]
        for (hd, d), cc, out in zip(chains, cs, outs):
            per_dir[d][3][hd, pl.ds(pl.multiple_of(cc * CHUNK, CHUNK), CHUNK), :] = out
        return tuple(new)

    if has_state:
        init = tuple((s0f_ref, s0b_ref)[d][0, hd] for hd, d in chains)
    else:
        init = tuple(jnp.zeros((DK, DV), F32) for _ in chains)
    final = lax.fori_loop(0, n, scan_step, init)
    if not has_state:
        for (hd, d), s in zip(chains, final):
            (sf_ref, sb_ref)[d][0, hd] = s

    def out_block(i, carry):
        r0 = pl.multiple_of(i * rb, rb)
        for hd in heads:
            o_ref[0, pl.ds(r0, rb), hd * DV:(hd + 1) * DV] = _rms(
                o_f[hd, pl.ds(r0, rb), :] + o_b[hd, pl.ds(r0, rb), :], on_ref[...])
        return carry

    lax.fori_loop(0, l // rb, out_block, 0)


def _gdn(qkv, ab, conv, a_log, dt_bias, o_norm, states):
    b, l, _ = qkv.shape
    n = l // CHUNK
    hh = B_HEADS
    ab4 = jnp.transpose(ab[:, :, :AB_W].reshape(b, l, 4, hh), (0, 3, 1, 2))
    abr = jnp.transpose(ab[:, :, :AB_W].reshape(b, n, CHUNK, 4, hh), (0, 4, 3, 1, 2))
    alog = jnp.transpose(a_log).reshape(hh, 2, 1, 1)
    dtb = jnp.transpose(dt_bias).reshape(hh, 2, 1, 1)
    has_state = states is not None
    hb = max(1, min(hh, GDN_GROUP // n))
    hg = hh // hb
    in_specs = [pl.BlockSpec((1, l, hb * DK), lambda i, h: (i, 0, h)),
                pl.BlockSpec((1, l, hb * DK), lambda i, h: (i, 0, hg + h)),
                pl.BlockSpec((1, l, hb * DV), lambda i, h: (i, 0, 2 * hg + h)),
                pl.BlockSpec((3, hb * DK), lambda i, h: (0, h)),
                pl.BlockSpec((3, hb * DK), lambda i, h: (0, hg + h)),
                pl.BlockSpec((3, hb * DV), lambda i, h: (0, 2 * hg + h)),
                pl.BlockSpec((1, hb, l, 4), lambda i, h: (i, h, 0, 0)),
                pl.BlockSpec((1, hb, 4, n, CHUNK), lambda i, h: (i, h, 0, 0, 0)),
                pl.BlockSpec((hb, 2, 1, 1), lambda i, h: (h, 0, 0, 0)),
                pl.BlockSpec((hb, 2, 1, 1), lambda i, h: (h, 0, 0, 0)),
                pl.BlockSpec((1, DV), lambda i, h: (0, 0))]
    args = [qkv, qkv, qkv, conv, conv, conv, ab4, abr, alog, dtb, o_norm.reshape(1, DV)]
    st_spec = pl.BlockSpec((1, hb, DK, DV), lambda i, h: (i, h, 0, 0))
    out_shape = [jax.ShapeDtypeStruct((b, l, hh * DV), F32)]
    out_specs = [pl.BlockSpec((1, l, hb * DV), lambda i, h: (i, 0, h))]
    if has_state:
        in_specs += [st_spec, st_spec]
        args += list(states)
    else:
        out_shape += [jax.ShapeDtypeStruct((b, hh, DK, DV), F32)] * 2
        out_specs += [st_spec, st_spec]
    per_dir = [pltpu.VMEM((hb, n * DK, DV), BF16), pltpu.VMEM((hb, n * DK, DV), F32),
               pltpu.VMEM((hb, l, DK), BF16), pltpu.VMEM((hb, l, DV), F32), pltpu.VMEM((hb, n * 8, DV), F32)]
    scratch = [pltpu.VMEM((hb, l, DK), F32), pltpu.VMEM((hb, l, DK), F32), pltpu.VMEM((hb, l, DV), F32)]
    scratch += per_dir + per_dir
    group = math.gcd(n, max(1, GDN_GROUP // hb))
    return pl.pallas_call(
        functools.partial(_gdn_kernel, has_state, l, hb, group),
        out_shape=out_shape, grid=(b, hg), in_specs=in_specs, out_specs=out_specs,
        scratch_shapes=scratch,
        compiler_params=_params("parallel", "parallel"),
        name="gated_deltanet",
    )(*args)


def _even_out_kernel(x_ref, sh_ref, sc_ref, gt_ref, ln_ref, at_ref, gd_ref, wz_ref, wo_ref, o_ref):
    x = x_ref[0]
    z = _mm(_adaln(x, ln_ref[...], sc_ref[0], sh_ref[0]), wz_ref[...])
    y = jnp.concatenate([at_ref[0], gd_ref[0]], axis=1) * _silu(z)
    o_ref[0] = x + gt_ref[0] * _mm(y, wo_ref[...])


def _even_out(x, shift, scale, gate, ln, attn, gdn, wz, wo, per_batch_cond):
    b, l, d = x.shape
    tm = min(l, 256)
    cidx = (lambda i, t: (i, 0, 0)) if per_batch_cond else (lambda i, t: (0, 0, 0))
    row = lambda i, t: (i, t, 0)
    return pl.pallas_call(
        _even_out_kernel,
        out_shape=jax.ShapeDtypeStruct((b, l, d), F32),
        grid=(b, l // tm),
        in_specs=[pl.BlockSpec((1, tm, d), row), pl.BlockSpec((1, 1, d), cidx), pl.BlockSpec((1, 1, d), cidx),
                  pl.BlockSpec((1, 1, d), cidx), _resident((1, d)),
                  pl.BlockSpec((1, tm, attn.shape[2]), row), pl.BlockSpec((1, tm, gdn.shape[2]), row),
                  _resident(wz.shape), _resident(wo.shape)],
        out_specs=pl.BlockSpec((1, tm, d), row),
        compiler_params=_params("parallel", "parallel"),
        name="even_out",
    )(x, shift, scale, gate, ln, attn, gdn, wz, wo)


def _pool_kernel(final, l, tm, xp_ref, x_ref, xn_ref, sh_ref, sc_ref, gt_ref, ln_ref, wp_ref, wz_ref,
                 wg_ref, ps_ref, wo_ref, fn_ref, o_ref):
    t = pl.program_id(1)
    x = x_ref[0]
    rows = tm + 2 * POOL_HALO
    xa = jnp.concatenate([xp_ref[0], x, xn_ref[0]], axis=0)
    h = _adaln(xa, ln_ref[...], sc_ref[0], sh_ref[0])
    ri = lax.broadcasted_iota(jnp.int32, (rows, 1), 0) + (t * tm - POOL_HALO)
    inside = jnp.logical_and(ri >= 0, ri < l)
    pin = jnp.where(inside, _mm(h, wp_ref[...]), 0.0)
    z = _mm(h[POOL_HALO:POOL_HALO + tm], wz_ref[...])
    pos = lax.broadcasted_iota(jnp.int32, (tm, 1), 0) + t * tm
    gw = pin.shape[1] // len(POOL_WINDOWS)
    mixed = []
    for gi, w in enumerate(POOL_WINDOWS):
        pg = pin[:, gi * gw:(gi + 1) * gw]
        acc = pg
        span = 1
        while span < w:
            acc = acc + pltpu.roll(acc, rows - span, 0)
            span *= 2
        first = POOL_HALO - w // 2
        win = pltpu.roll(acc, rows - first, 0)[:tm] if first else acc[:tm]
        cnt = (jnp.minimum(pos + (w - w // 2), l) - jnp.maximum(pos - w // 2, 0)).astype(F32)
        pooled = win / cnt - pg[POOL_HALO:POOL_HALO + tm]
        mixed.append(_mm(pooled, wg_ref[gi]))
    y = jnp.concatenate(mixed, axis=1) * ps_ref[...] * _silu(z)
    out = x + gt_ref[0] * _mm(y, wo_ref[...])
    o_ref[0] = _rms(out, fn_ref[...]) if final else out


def _pool_layer(x, shift, scale, gate, ln, wp, wz, wg, ps, wo, fnorm, per_batch_cond, final):
    b, l, d = x.shape
    tm = min(l, 256)
    hb = tm // POOL_HALO
    nb = l // POOL_HALO
    cidx = (lambda i, t: (i, 0, 0)) if per_batch_cond else (lambda i, t: (0, 0, 0))
    row = lambda i, t: (i, t, 0)
    return pl.pallas_call(
        functools.partial(_pool_kernel, final, l, tm),
        out_shape=jax.ShapeDtypeStruct((b, l, d), F32),
        grid=(b, l // tm),
        in_specs=[pl.BlockSpec((1, POOL_HALO, d), lambda i, t: (i, jnp.maximum(t * hb - 1, 0), 0)),
                  pl.BlockSpec((1, tm, d), row),
                  pl.BlockSpec((1, POOL_HALO, d), lambda i, t: (i, jnp.minimum((t + 1) * hb, nb - 1), 0)),
                  pl.BlockSpec((1, 1, d), cidx), pl.BlockSpec((1, 1, d), cidx), pl.BlockSpec((1, 1, d), cidx),
                  _resident((1, d)), _resident(wp.shape), _resident(wz.shape), _resident(wg.shape),
                  _resident((1, d)), _resident(wo.shape), _resident((1, d))],
        out_specs=pl.BlockSpec((1, tm, d), row),
        compiler_params=_params("parallel", "parallel"),
        name="pool_mixer",
    )(x, x, x, shift, scale, gate, ln, wp, wz, wg, ps, wo, fnorm)


def _rope_tables(n_tokens):
    rows = n_tokens // GRID_W
    row_pos = jnp.repeat(jnp.arange(rows), GRID_W).astype(F32)
    col_pos = jnp.tile(jnp.arange(GRID_W), rows).astype(F32)
    half = QK_ROPE // 2
    inv_freq = ROPE_BASE ** (-jnp.arange(0, half, 2, dtype=F32) / half)
    ar = row_pos[:, None] * inv_freq
    ac = col_pos[:, None] * inv_freq
    cos = jnp.concatenate([jnp.cos(ar), jnp.cos(ar), jnp.cos(ac), jnp.cos(ac)], axis=1)
    sin = jnp.concatenate([-jnp.sin(ar), jnp.sin(ar), -jnp.sin(ac), jnp.sin(ac)], axis=1)
    return jnp.concatenate([cos, cos], axis=1), jnp.concatenate([sin, sin], axis=1)


def _swap_perm():
    q = QK_ROPE // 4
    return jnp.array(list(range(q, 2 * q)) + list(range(q)) + list(range(3 * q, 4 * q)) + list(range(2 * q, 3 * q)))


def _even_weights(w_in, w_uq, w_ukv, w_out):
    d = w_in.shape[0]
    perm = _swap_perm()
    o = 0
    cq = w_in[:, o:o + D_CQ]; o += D_CQ
    ckv = w_in[:, o:o + D_CKV]; o += D_CKV
    kpe = w_in[:, o:o + QK_ROPE]; o += QK_ROPE
    qkv = w_in[:, o:o + QKV_W]; o += QKV_W
    ab = w_in[:, o:o + AB_W]; o += AB_W
    wz = w_in[:, o:]
    pad = lambda n: jnp.zeros((d, n), w_in.dtype)
    w = jnp.concatenate([cq, ckv, kpe, pad(LANES - QK_ROPE), kpe[:, perm], pad(LANES - QK_ROPE),
                         ab, pad(LANES - AB_W), qkv], axis=1).astype(BF16)
    uq = w_uq.reshape(D_CQ, A_HEADS, QK_NOPE + QK_ROPE)
    pe = uq[:, :, QK_NOPE:]
    wuq = jnp.concatenate([uq[:, :, :QK_NOPE].reshape(D_CQ, -1), pe.reshape(D_CQ, -1),
                           pe[:, :, perm].reshape(D_CQ, -1)], axis=1).astype(BF16)
    return w, wuq, w_ukv.astype(BF16), wz.astype(BF16), w_out.astype(BF16)


def kernel(x_prompt, x_sample, cache_ckv, cache_kpe, state_fwd, state_bwd, c, c_ctx, ln_e, mod_w_e, mod_b_e, w_in_e, q_norm_e, kv_norm_e, w_uq_e, w_ukv_e, conv_e, a_log_e, dt_bias_e, o_norm_e, w_out_e, ln_o, mod_w_o, mod_b_o, w_in_o, w_pool_o, pool_scale_o, w_out_o, final_norm):
    d = x_prompt.shape[-1]
    nb = c.shape[0]
    depth = ln_e.shape[0] + ln_o.shape[0]
    assert depth % 2 == 0, "the final norm is fused into the last (pooling) layer"
    cond = jnp.concatenate([c_ctx[None, :], c, jnp.zeros((COND_ROWS - 1 - nb, d), F32)], axis=0)
    mod = {0: _modulation(cond, mod_w_e, mod_b_e), 1: _modulation(cond, mod_w_o, mod_b_o)}
    rope = _rope_tables(x_sample.shape[1])
    fnorm = final_norm.reshape(1, d)
    xp, xs = x_prompt, x_sample
    ckv_out, kpe_out, sf_out, sb_out = [], [], [], []
    for layer in range(depth):
        i = layer // 2
        m = mod[layer % 2][i]
        sh_p, sc_p, gt_p = (m[0:1, j * d:(j + 1) * d].reshape(1, 1, d) for j in range(3))
        sh_s, sc_s, gt_s = (m[1:1 + nb, j * d:(j + 1) * d].reshape(nb, 1, d) for j in range(3))
        if layer % 2 == 0:
            ln = ln_e[i].reshape(1, d)
            w, wuq, wukv, wz, wo = _even_weights(w_in_e[i], w_uq_e[i], w_ukv_e[i], w_out_e[i])
            qg, kg = q_norm_e[i].reshape(1, D_CQ), kv_norm_e[i].reshape(1, D_CKV)
            gdn_w = (conv_e[i], a_log_e[i], dt_bias_e[i], o_norm_e[i])
            q, kv, kper, qkv, ab, ckv, kpe = _even_in(xp, sh_p, sc_p, ln, w, qg, kg, wuq, wukv, None, False, True)
            attn = _attention(q, kv, kper)
            g_o, s_f, s_b = _gdn(qkv, ab, *gdn_w, None)
            xp = _even_out(xp, sh_p, sc_p, gt_p, ln, attn, g_o, wz, wo, False)
            ckv_out.append(ckv); kpe_out.append(kpe); sf_out.append(s_f); sb_out.append(s_b)
            q, kv, kper, qkv, ab = _even_in(xs, sh_s, sc_s, ln, w, qg, kg, wuq, wukv, rope, True, False)
            attn = _attention(q, kv, kper, (_kv_up(cache_ckv[:, i], wukv), cache_kpe[:, i]))
            (g_o,) = _gdn(qkv, ab, *gdn_w, (state_fwd[:, i], state_bwd[:, i]))
            xs = _even_out(xs, sh_s, sc_s, gt_s, ln, attn, g_o, wz, wo, True)
        else:
            ln = ln_o[i].reshape(1, d)
            wp = w_in_o[i][:, :d].astype(BF16)
            wzo = w_in_o[i][:, d:].astype(BF16)
            wg = w_pool_o[i].astype(BF16)
            ps = pool_scale_o[i].reshape(1, d)
            wo = w_out_o[i].astype(BF16)
            final = layer == depth - 1
            xp = _pool_layer(xp, sh_p, sc_p, gt_p, ln, wp, wzo, wg, ps, wo, fnorm, False, final)
            xs = _pool_layer(xs, sh_s, sc_s, gt_s, ln, wp, wzo, wg, ps, wo, fnorm, True, final)
    return (xp, xs, jnp.stack(ckv_out, axis=1), jnp.stack(kpe_out, axis=1),
            jnp.stack(sf_out, axis=1), jnp.stack(sb_out, axis=1))
```

```python
import functools
import math

import jax
import jax.numpy as jnp
from jax import lax
from jax.experimental import pallas as pl
from jax.experimental.pallas import tpu as pltpu

F32 = jnp.float32
BF16 = jnp.bfloat16

EPS = 1e-6
GRID_W = 64
ROPE_BASE = 10000.0
A_HEADS = 8
QK_NOPE = 128
QK_ROPE = 64
V_HEAD = 128
D_CQ = 512
D_CKV = 512
B_HEADS = 8
DK = 128
DV = 128
CHUNK = 64
INV_BLOCK = 16
GDN_GROUP = 16
ATTN_BLOCK = 256
ATTN_TILE = 2048
POOL_WINDOWS = (2, 4, 8, 16)
POOL_HALO = 8
CONV_HALO = 8
QKV_W = B_HEADS * (2 * DK + DV)
AB_W = 4 * B_HEADS
LANES = 128
V7X_VMEM_BYTES = 64 * 1024 * 1024
VMEM_LIMIT = V7X_VMEM_BYTES - 8 * 1024 * 1024
COND_ROWS = 16

_C_CQ = 0
_C_CKV = D_CQ
_C_KPE = D_CQ + D_CKV
_C_KPE_SW = _C_KPE + LANES
_C_AB = _C_KPE_SW + LANES
_C_QKV = _C_AB + LANES
IN_W = _C_QKV + QKV_W


def _mm(a, b):
    return jnp.dot(a.astype(BF16), b.astype(BF16), preferred_element_type=F32)


def _mm_nt(a, b):
    return lax.dot_general(a.astype(BF16), b.astype(BF16), (((1,), (1,)), ((), ())),
                           preferred_element_type=F32)


def _mm_tn(a, b):
    return lax.dot_general(a.astype(BF16), b.astype(BF16), (((0,), (0,)), ((), ())),
                           preferred_element_type=F32)


def _sigmoid(x):
    return 1.0 / (1.0 + jnp.exp(-x))


def _silu(x):
    return x * _sigmoid(x)


def _softplus(x):
    return jnp.maximum(x, 0.0) + jnp.log(1.0 + jnp.exp(-jnp.abs(x)))


def _rms(x, g):
    return x * lax.rsqrt(jnp.mean(x * x, axis=-1, keepdims=True) + EPS) * g


def _adaln(x, ln, sc, sh):
    return _rms(x, ln) * (1.0 + sc) + sh


def _params(*sem):
    return pltpu.CompilerParams(dimension_semantics=sem, vmem_limit_bytes=VMEM_LIMIT)


def _resident(shape):
    nd = len(shape)
    return pl.BlockSpec(shape, lambda *_: (0,) * nd, pipeline_mode=pl.Buffered(1))


def _mod_kernel(c_ref, w_ref, b_ref, o_ref):
    o_ref[0] = _mm(_silu(c_ref[...]), w_ref[0]) + b_ref[0]


def _modulation(cond, w, b):
    n, d, e = w.shape
    tn = 1024
    return pl.pallas_call(
        _mod_kernel,
        out_shape=jax.ShapeDtypeStruct((n, COND_ROWS, e), F32),
        grid=(n, e // tn),
        in_specs=[pl.BlockSpec((COND_ROWS, d), lambda l, j: (0, 0)),
                  pl.BlockSpec((1, d, tn), lambda l, j: (l, 0, j)),
                  pl.BlockSpec((1, 1, tn), lambda l, j: (l, 0, j))],
        out_specs=pl.BlockSpec((1, COND_ROWS, tn), lambda l, j: (l, 0, j)),
        compiler_params=_params("parallel", "parallel"),
        name="modulation",
    )(cond, w, b.reshape(n, 1, e))


def _even_in_kernel(rope, emit_cache, l, tm, xp_ref, x_ref, xn_ref, sh_ref, sc_ref, ln_ref, w_ref, cw_ref,
                    qg_ref, kg_ref, wuq_ref, wukv_ref, *rest):
    if rope:
        cos_ref, sin_ref, *outs = rest
    else:
        outs = rest
    if emit_cache:
        q_ref, kv_ref, kper_ref, qkv_ref, ab_ref, ckv_ref, kpe_ref = outs
    else:
        q_ref, kv_ref, kper_ref, qkv_ref, ab_ref = outs
    t = pl.program_id(1)
    rows = tm + 2 * CONV_HALO
    xa = jnp.concatenate([xp_ref[0], x_ref[0], xn_ref[0]], axis=0)
    ha = _adaln(xa, ln_ref[...], sc_ref[0], sh_ref[0])
    hab = ha.astype(BF16)

    ri = lax.broadcasted_iota(jnp.int32, (rows, 1), 0) + (t * tm - CONV_HALO)
    inside = jnp.logical_and(ri >= 0, ri < l)
    gw = B_HEADS * DK
    for part, unit_scale in enumerate((DK ** -0.5, 1.0, None)):
        cols = slice(part * gw, (part + 1) * gw)
        pre = jnp.where(inside, _mm(hab, w_ref[:, _C_QKV + part * gw:_C_QKV + (part + 1) * gw]), 0.0)
        conv = (cw_ref[0:1, cols] * pltpu.roll(pre, 1, 0) + cw_ref[1:2, cols] * pre
                + cw_ref[2:3, cols] * pltpu.roll(pre, rows - 1, 0))
        act = _silu(conv[CONV_HALO:CONV_HALO + tm])
        if unit_scale is None:
            qkv_ref[0, :, cols] = act
        else:
            for g in range(B_HEADS):
                blk = act[:, g * DK:(g + 1) * DK]
                unit = lax.rsqrt(jnp.sum(blk * blk, axis=-1, keepdims=True) + EPS) * unit_scale
                qkv_ref[0, :, part * gw + g * DK:part * gw + (g + 1) * DK] = blk * unit

    p = _mm(ha[CONV_HALO:CONV_HALO + tm], w_ref[:, :_C_QKV])
    kpe = p[:, _C_KPE:_C_KPE + QK_ROPE]
    ckv = _rms(p[:, _C_CKV:_C_CKV + D_CKV], kg_ref[...])
    qq = _mm(_rms(p[:, _C_CQ:_C_CQ + D_CQ], qg_ref[...]), wuq_ref[...])
    n_nope = A_HEADS * QK_NOPE
    n_pe = A_HEADS * QK_ROPE
    qpe = qq[:, n_nope:n_nope + n_pe]
    if rope:
        cos = cos_ref[...]
        sin = sin_ref[...]
        kpe = kpe * cos[:, :QK_ROPE] + p[:, _C_KPE_SW:_C_KPE_SW + QK_ROPE] * sin[:, :QK_ROPE]
        reps = n_pe // LANES
        qpe = (qpe * jnp.concatenate([cos] * reps, axis=1)
               + qq[:, n_nope + n_pe:] * jnp.concatenate([sin] * reps, axis=1))
    for hd in range(A_HEADS):
        q_ref[0, hd, :, :QK_NOPE] = qq[:, hd * QK_NOPE:(hd + 1) * QK_NOPE].astype(BF16)
        q_ref[0, hd, :, QK_NOPE:] = qpe[:, hd * QK_ROPE:(hd + 1) * QK_ROPE].astype(BF16)
    kv_ref[0] = _mm(ckv, wukv_ref[...]).astype(BF16)
    kper_ref[0] = kpe.astype(BF16)
    ab_ref[0] = p[:, _C_AB:_C_AB + LANES]
    if emit_cache:
        ckv_ref[0] = ckv
        kpe_ref[0] = p[:, _C_KPE:_C_KPE + QK_ROPE]


def _even_in(x, shift, scale, ln, w, conv, qg, kg, wuq, wukv, rope_tabs, per_batch_cond, emit_cache):
    b, l, d = x.shape
    tm = min(l, 256)
    hb = tm // CONV_HALO
    nb = l // CONV_HALO
    cidx = (lambda i, t: (i, 0, 0)) if per_batch_cond else (lambda i, t: (0, 0, 0))
    in_specs = [pl.BlockSpec((1, CONV_HALO, d), lambda i, t: (i, jnp.maximum(t * hb - 1, 0), 0)),
                pl.BlockSpec((1, tm, d), lambda i, t: (i, t, 0)),
                pl.BlockSpec((1, CONV_HALO, d), lambda i, t: (i, jnp.minimum((t + 1) * hb, nb - 1), 0)),
                pl.BlockSpec((1, 1, d), cidx),
                pl.BlockSpec((1, 1, d), cidx),
                _resident((1, d)), _resident(w.shape), _resident(conv.shape), _resident((1, D_CQ)),
                _resident((1, D_CKV)), _resident(wuq.shape), _resident(wukv.shape)]
    args = [x, x, x, shift, scale, ln, w, conv, qg, kg, wuq, wukv]
    if rope_tabs is not None:
        in_specs += [pl.BlockSpec((tm, LANES), lambda i, t: (t, 0))] * 2
        args += list(rope_tabs)
    kvw = A_HEADS * (QK_NOPE + V_HEAD)
    out_shape = [jax.ShapeDtypeStruct((b, A_HEADS, l, QK_NOPE + QK_ROPE), BF16),
                 jax.ShapeDtypeStruct((b, l, kvw), BF16),
                 jax.ShapeDtypeStruct((b, l, QK_ROPE), BF16),
                 jax.ShapeDtypeStruct((b, l, QKV_W), F32),
                 jax.ShapeDtypeStruct((b, l, LANES), F32)]
    out_specs = [pl.BlockSpec((1, A_HEADS, tm, QK_NOPE + QK_ROPE), lambda i, t: (i, 0, t, 0)),
                 pl.BlockSpec((1, tm, kvw), lambda i, t: (i, t, 0)),
                 pl.BlockSpec((1, tm, QK_ROPE), lambda i, t: (i, t, 0)),
                 pl.BlockSpec((1, tm, QKV_W), lambda i, t: (i, t, 0)),
                 pl.BlockSpec((1, tm, LANES), lambda i, t: (i, t, 0))]
    if emit_cache:
        out_shape += [jax.ShapeDtypeStruct((b, l, D_CKV), F32), jax.ShapeDtypeStruct((b, l, QK_ROPE), F32)]
        out_specs += [pl.BlockSpec((1, tm, D_CKV), lambda i, t: (i, t, 0)),
                      pl.BlockSpec((1, tm, QK_ROPE), lambda i, t: (i, t, 0))]
    return pl.pallas_call(
        functools.partial(_even_in_kernel, rope_tabs is not None, emit_cache, l, tm),
        out_shape=out_shape, grid=(b, l // tm), in_specs=in_specs, out_specs=out_specs,
        compiler_params=_params("parallel", "parallel"),
        name="even_in",
    )(*args)


def _kv_up_kernel(c_ref, w_ref, o_ref):
    o_ref[0] = _mm(c_ref[0], w_ref[...]).astype(BF16)


def _kv_up(ckv, wukv):
    b, p, c = ckv.shape
    e = wukv.shape[1]
    return pl.pallas_call(
        _kv_up_kernel,
        out_shape=jax.ShapeDtypeStruct((b, p, e), BF16),
        grid=(b,),
        in_specs=[pl.BlockSpec((1, p, c), lambda i: (i, 0, 0)), _resident(wukv.shape)],
        out_specs=pl.BlockSpec((1, p, e), lambda i: (i, 0, 0)),
        compiler_params=_params("parallel"),
        name="kv_up",
    )(ckv, wukv)


def _attn_kernel(n_ctx, hb, th, *refs):
    if n_ctx:
        q_ref, kvc_ref, pec_ref, kvl_ref, pel_ref, o_ref, kcat, vcat = refs
    else:
        q_ref, kvl_ref, pel_ref, o_ref, kcat, vcat = refs
    hw = QK_NOPE + V_HEAD

    @pl.when(pl.program_id(2) == 0)
    def _():
        for hd in range(hb):
            if n_ctx:
                kcat[hd, :n_ctx, :QK_NOPE] = kvc_ref[0, :, hd * hw:hd * hw + QK_NOPE]
                kcat[hd, :n_ctx, QK_NOPE:] = pec_ref[0].astype(BF16)
                vcat[hd, :n_ctx, :] = kvc_ref[0, :, hd * hw + QK_NOPE:(hd + 1) * hw]
            kcat[hd, n_ctx:, :QK_NOPE] = kvl_ref[0, :, hd * hw:hd * hw + QK_NOPE]
            kcat[hd, n_ctx:, QK_NOPE:] = pel_ref[0]
            vcat[hd, n_ctx:, :] = kvl_ref[0, :, hd * hw + QK_NOPE:(hd + 1) * hw]

    c = (QK_NOPE + QK_ROPE) ** -0.5 * math.log2(math.e)
    blocks = [(hd, j) for hd in range(hb) for j in range(q_ref.shape[2] // th)]

    def scores(hd, j):
        return _mm_nt(q_ref[0, hd, j * th:(j + 1) * th, :], kcat[hd])

    ahead = 2
    pending = [scores(*blk) for blk in blocks[:ahead]]
    for i, (hd, j) in enumerate(blocks):
        s = pending.pop(0)
        e = jnp.exp2((s - jnp.max(s, axis=-1, keepdims=True)) * c)
        o = _mm(e, vcat[hd])
        if i + ahead < len(blocks):
            pending.append(scores(*blocks[i + ahead]))
        o_ref[0, j * th:(j + 1) * th, hd * V_HEAD:(hd + 1) * V_HEAD] = o * (1.0 / jnp.sum(e, axis=-1, keepdims=True))


def _attention(q, kv, kpe, ctx=None):
    b, hh, l, dq = q.shape
    th = min(l, ATTN_BLOCK)
    tq = min(l, ATTN_TILE)
    hb = min(hh, ATTN_TILE // tq)
    hw = QK_NOPE + V_HEAD
    n_ctx = 0 if ctx is None else ctx[0].shape[1]
    lk = n_ctx + l
    in_specs = [pl.BlockSpec((1, hb, tq, dq), lambda i, h, t: (i, h, t, 0))]
    args = [q]
    if ctx is not None:
        in_specs += [pl.BlockSpec((1, n_ctx, hb * hw), lambda i, h, t: (i, 0, h)),
                     pl.BlockSpec((1, n_ctx, QK_ROPE), lambda i, h, t: (i, 0, 0))]
        args += list(ctx)
    in_specs += [pl.BlockSpec((1, l, hb * hw), lambda i, h, t: (i, 0, h)),
                 pl.BlockSpec((1, l, QK_ROPE), lambda i, h, t: (i, 0, 0))]
    args += [kv, kpe]
    return pl.pallas_call(
        functools.partial(_attn_kernel, n_ctx, hb, th),
        out_shape=jax.ShapeDtypeStruct((b, l, hh * V_HEAD), F32),
        grid=(b, hh // hb, l // tq),
        in_specs=in_specs,
        out_specs=pl.BlockSpec((1, tq, hb * V_HEAD), lambda i, h, t: (i, t, h)),
        scratch_shapes=[pltpu.VMEM((hb, lk, dq), BF16), pltpu.VMEM((hb, lk, V_HEAD), BF16)],
        compiler_params=_params("parallel", "parallel", "arbitrary"),
        name="attention",
    )(*args)


def _unit_triangular_inverses(mats, rowi, coli):
    eye = (rowi == coli).astype(F32)
    same = (rowi // INV_BLOCK) == (coli // INV_BLOCK)
    ps = [jnp.where(same, a, 0.0) for a in mats]
    ts = [eye - p for p in ps]
    for _ in range(int(math.log2(INV_BLOCK)) - 1):
        ps = [_mm(p, p) for p in ps]
        ts = [t + _mm(t, p) for t, p in zip(ts, ps)]
    size = INV_BLOCK
    while size < CHUNK:
        wider = (rowi // (2 * size)) == (coli // (2 * size))
        pick = jnp.logical_and(wider, jnp.logical_not(same))
        cts = [_mm(jnp.where(pick, a, 0.0), t) for a, t in zip(mats, ts)]
        ts = [t - _mm(t, ct) for t, ct in zip(ts, cts)]
        same = wider
        size *= 2
    return ts


def _gdn_kernel(has_state, l, hb, group, qr_ref, kr_ref, vr_ref, abc_ref, abr_ref, alog_ref, dtb_ref, on_ref, *rest):
    if has_state:
        s0f_ref, s0b_ref, o_ref = rest[:3]
        scr = rest[3:]
        sf_ref = sb_ref = None
    else:
        o_ref, sf_ref, sb_ref = rest[:3]
        scr = rest[3:]
    (km_f, nm_f, qp_f, o_f, dc_f, km_b, nm_b, qp_b, o_b, dc_b) = scr
    per_dir = ((km_f, nm_f, qp_f, o_f, dc_f), (km_b, nm_b, qp_b, o_b, dc_b))
    n = l // CHUNK
    rb = min(l, 512)
    heads = range(hb)

    neg_a = [[-jnp.exp(alog_ref[hd, d]) for d in (0, 1)] for hd in heads]
    dt = [[dtb_ref[hd, d] for d in (0, 1)] for hd in heads]
    rowi = lax.broadcasted_iota(jnp.int32, (CHUNK, CHUNK), 0)
    coli = lax.broadcasted_iota(jnp.int32, (CHUNK, CHUNK), 1)
    lower = coli <= rowi
    upper = coli >= rowi

    def load_chunk(hd, c):
        r0 = pl.multiple_of(c * CHUNK, CHUNK)
        lanes = slice(hd * DK, (hd + 1) * DK)
        return (qr_ref[0, pl.ds(r0, CHUNK), lanes], kr_ref[0, pl.ds(r0, CHUNK), lanes],
                vr_ref[0, pl.ds(r0, CHUNK), lanes],
                abc_ref[0, hd, pl.ds(r0, CHUNK), :],
                [abr_ref[0, hd, d, pl.ds(c, 1), :] for d in (0, 1)])

    def gates(hd, d, k, v, abc, abr):
        tri, tri_t = (lower, upper) if d == 0 else (upper, lower)
        g_col = neg_a[hd][d] * _softplus(abc[:, d:d + 1] + dt[hd][d])
        g_row = neg_a[hd][d] * _softplus(abr + dt[hd][d])
        beta = _sigmoid(abc[:, 2 + d:3 + d])
        gc = jnp.sum(jnp.where(tri, jnp.broadcast_to(g_row, (CHUNK, CHUNK)), 0.0), axis=1, keepdims=True)
        gr = jnp.sum(jnp.where(tri_t, jnp.broadcast_to(g_col, (CHUNK, CHUNK)), 0.0), axis=0, keepdims=True)
        gamma = jnp.exp(jnp.where(tri, gc - gr, -jnp.inf))
        kb = k * beta
        e_gc = jnp.exp(gc)
        g_end = gc[CHUNK - 1:CHUNK, :] if d == 0 else gc[0:1, :]
        return dict(kb=kb, gamma=gamma, strict=jnp.where(rowi == coli, 0.0, gamma), e_gc=e_gc,
                    rhs=jnp.concatenate([v * beta, kb * e_gc], axis=1),
                    kd=(k * jnp.exp(g_end - gc)).astype(BF16), dec=jnp.broadcast_to(jnp.exp(g_end), (8, DV)))

    def chunk_group(i, carry):
        where = [(hd, i * group + j) for hd in heads for j in range(group)]
        loaded = [load_chunk(hd, c) for hd, c in where]
        both = [[gates(hd, d, k, v, abc, abr[d]) for d in (0, 1)]
                for (hd, _), (q, k, v, abc, abr) in zip(where, loaded)]
        stacked = [_mm_nt(jnp.concatenate([q, g[0]["kb"], g[1]["kb"]], axis=0), k)
                   for (q, k, *_), g in zip(loaded, both)]
        items = [dict(g[d], q=ld[0], qk=st[:CHUNK], kk=st[(1 + d) * CHUNK:(2 + d) * CHUNK], pos=pos, d=d)
                 for pos, ld, g, st in zip(where, loaded, both, stacked) for d in (0, 1)]
        invs = _unit_triangular_inverses([it["kk"] * it["strict"] for it in items], rowi, coli)
        sols = [_mm(t, it["rhs"]).astype(BF16) for t, it in zip(invs, items)]
        intras = [(it["qk"] * it["gamma"]).astype(BF16) for it in items]
        kd_sol = [_mm_tn(it["kd"], s) for it, s in zip(items, sols)]
        in_sol = [_mm(x, s) for x, s in zip(intras, sols)]
        for it, ks_, is_ in zip(items, kd_sol, in_sol):
            hd, c = it["pos"]
            km_s, nm_s, qp_s, o_s, dc_s = per_dir[it["d"]]
            r0 = pl.multiple_of(c * CHUNK, CHUNK)
            m0 = pl.multiple_of(c * DK, DK)
            nm_s[hd, pl.ds(m0, DK), :] = ks_[:, :DV]
            km_s[hd, pl.ds(m0, DK), :] = ks_[:, DV:].astype(BF16)
            o_s[hd, pl.ds(r0, CHUNK), :] = is_[:, :DV]
            qp_s[hd, pl.ds(r0, CHUNK), :] = (it["q"] * it["e_gc"] - is_[:, DV:]).astype(BF16)
            dc_s[hd, pl.ds(pl.multiple_of(c * 8, 8), 8), :] = it["dec"]
        return carry

    lax.fori_loop(0, n // group, chunk_group, 0)

    def load_step(hd, d, c):
        km_s, nm_s, qp_s, o_s, dc_s = per_dir[d]
        r0 = pl.multiple_of(c * CHUNK, CHUNK)
        m0 = pl.multiple_of(c * DK, DK)
        return (km_s[hd, pl.ds(m0, DK), :], nm_s[hd, pl.ds(m0, DK), :], qp_s[hd, pl.ds(r0, CHUNK), :],
                o_s[hd, pl.ds(r0, CHUNK), :], dc_s[hd, pl.ds(pl.multiple_of(c * 8, 8), 8), :][0:1, :])

    chains = [(hd, d) for hd in heads for d in (0, 1)]

    def scan_step(c, states):
        cs = [c if d == 0 else n - 1 - c for _, d in chains]
        loaded = [load_step(hd, d, cc) for (hd, d), cc in zip(chains, cs)]
        sbs = [s.astype(BF16) for s in states]
        new = [s * dec + nm - _mm(km, sb) for (km, nm, qp, o_loc, dec), s, sb in zip(loaded, states, sbs)]
        outs = [o_loc + _mm(qp, sb) for (km, nm, qp, o_loc, dec), sb in zip(loaded, sbs)]
        for (hd, d), cc, out in zip(chains, cs, outs):
            per_dir[d][3][hd, pl.ds(pl.multiple_of(cc * CHUNK, CHUNK), CHUNK), :] = out
        return tuple(new)

    if has_state:
        init = tuple((s0f_ref, s0b_ref)[d][0, hd] for hd, d in chains)
    else:
        init = tuple(jnp.zeros((DK, DV), F32) for _ in chains)
    final = lax.fori_loop(0, n, scan_step, init)
    if not has_state:
        for (hd, d), s in zip(chains, final):
            (sf_ref, sb_ref)[d][0, hd] = s

    def out_block(i, carry):
        r0 = pl.multiple_of(i * rb, rb)
        for hd in heads:
            o_ref[0, pl.ds(r0, rb), hd * DV:(hd + 1) * DV] = _rms(
                o_f[hd, pl.ds(r0, rb), :] + o_b[hd, pl.ds(r0, rb), :], on_ref[...])
        return carry

    lax.fori_loop(0, l // rb, out_block, 0)


def _gdn(qkv, ab, a_log, dt_bias, o_norm, states):
    b, l, _ = qkv.shape
    n = l // CHUNK
    hh = B_HEADS
    ab4 = jnp.transpose(ab[:, :, :AB_W].reshape(b, l, 4, hh), (0, 3, 1, 2))
    abr = jnp.transpose(ab[:, :, :AB_W].reshape(b, n, CHUNK, 4, hh), (0, 4, 3, 1, 2))
    alog = jnp.transpose(a_log).reshape(hh, 2, 1, 1)
    dtb = jnp.transpose(dt_bias).reshape(hh, 2, 1, 1)
    has_state = states is not None
    hb = max(1, min(hh, GDN_GROUP // n))
    hg = hh // hb
    in_specs = [pl.BlockSpec((1, l, hb * DK), lambda i, h: (i, 0, h)),
                pl.BlockSpec((1, l, hb * DK), lambda i, h: (i, 0, hg + h)),
                pl.BlockSpec((1, l, hb * DV), lambda i, h: (i, 0, 2 * hg + h)),
                pl.BlockSpec((1, hb, l, 4), lambda i, h: (i, h, 0, 0)),
                pl.BlockSpec((1, hb, 4, n, CHUNK), lambda i, h: (i, h, 0, 0, 0)),
                pl.BlockSpec((hb, 2, 1, 1), lambda i, h: (h, 0, 0, 0)),
                pl.BlockSpec((hb, 2, 1, 1), lambda i, h: (h, 0, 0, 0)),
                pl.BlockSpec((1, DV), lambda i, h: (0, 0))]
    args = [qkv, qkv, qkv, ab4, abr, alog, dtb, o_norm.reshape(1, DV)]
    st_spec = pl.BlockSpec((1, hb, DK, DV), lambda i, h: (i, h, 0, 0))
    out_shape = [jax.ShapeDtypeStruct((b, l, hh * DV), F32)]
    out_specs = [pl.BlockSpec((1, l, hb * DV), lambda i, h: (i, 0, h))]
    if has_state:
        in_specs += [st_spec, st_spec]
        args += list(states)
    else:
        out_shape += [jax.ShapeDtypeStruct((b, hh, DK, DV), F32)] * 2
        out_specs += [st_spec, st_spec]
    per_dir = [pltpu.VMEM((hb, n * DK, DV), BF16), pltpu.VMEM((hb, n * DK, DV), F32),
               pltpu.VMEM((hb, l, DK), BF16), pltpu.VMEM((hb, l, DV), F32), pltpu.VMEM((hb, n * 8, DV), F32)]
    scratch = per_dir + per_dir
    group = math.gcd(n, max(1, GDN_GROUP // hb))
    return pl.pallas_call(
        functools.partial(_gdn_kernel, has_state, l, hb, group),
        out_shape=out_shape, grid=(b, hg), in_specs=in_specs, out_specs=out_specs,
        scratch_shapes=scratch,
        compiler_params=_params("parallel", "parallel"),
        name="gated_deltanet",
    )(*args)


def _even_out_kernel(x_ref, sh_ref, sc_ref, gt_ref, ln_ref, at_ref, gd_ref, wz_ref, wo_ref, o_ref):
    x = x_ref[0]
    z = _mm(_adaln(x, ln_ref[...], sc_ref[0], sh_ref[0]), wz_ref[...])
    y = jnp.concatenate([at_ref[0], gd_ref[0]], axis=1) * _silu(z)
    o_ref[0] = x + gt_ref[0] * _mm(y, wo_ref[...])


def _even_out(x, shift, scale, gate, ln, attn, gdn, wz, wo, per_batch_cond):
    b, l, d = x.shape
    tm = min(l, 256)
    cidx = (lambda i, t: (i, 0, 0)) if per_batch_cond else (lambda i, t: (0, 0, 0))
    row = lambda i, t: (i, t, 0)
    return pl.pallas_call(
        _even_out_kernel,
        out_shape=jax.ShapeDtypeStruct((b, l, d), F32),
        grid=(b, l // tm),
        in_specs=[pl.BlockSpec((1, tm, d), row), pl.BlockSpec((1, 1, d), cidx), pl.BlockSpec((1, 1, d), cidx),
                  pl.BlockSpec((1, 1, d), cidx), _resident((1, d)),
                  pl.BlockSpec((1, tm, attn.shape[2]), row), pl.BlockSpec((1, tm, gdn.shape[2]), row),
                  _resident(wz.shape), _resident(wo.shape)],
        out_specs=pl.BlockSpec((1, tm, d), row),
        compiler_params=_params("parallel", "parallel"),
        name="even_out",
    )(x, shift, scale, gate, ln, attn, gdn, wz, wo)


def _pool_kernel(final, l, tm, xp_ref, x_ref, xn_ref, sh_ref, sc_ref, gt_ref, ln_ref, wp_ref, wz_ref,
                 wg_ref, ps_ref, wo_ref, fn_ref, o_ref):
    t = pl.program_id(1)
    x = x_ref[0]
    rows = tm + 2 * POOL_HALO
    xa = jnp.concatenate([xp_ref[0], x, xn_ref[0]], axis=0)
    h = _adaln(xa, ln_ref[...], sc_ref[0], sh_ref[0])
    ri = lax.broadcasted_iota(jnp.int32, (rows, 1), 0) + (t * tm - POOL_HALO)
    inside = jnp.logical_and(ri >= 0, ri < l)
    pin = jnp.where(inside, _mm(h, wp_ref[...]), 0.0)
    z = _mm(h[POOL_HALO:POOL_HALO + tm], wz_ref[...])
    pos = lax.broadcasted_iota(jnp.int32, (tm, 1), 0) + t * tm
    gw = pin.shape[1] // len(POOL_WINDOWS)
    mixed = []
    for gi, w in enumerate(POOL_WINDOWS):
        pg = pin[:, gi * gw:(gi + 1) * gw]
        acc = pg
        span = 1
        while span < w:
            acc = acc + pltpu.roll(acc, rows - span, 0)
            span *= 2
        first = POOL_HALO - w // 2
        win = pltpu.roll(acc, rows - first, 0)[:tm] if first else acc[:tm]
        cnt = (jnp.minimum(pos + (w - w // 2), l) - jnp.maximum(pos - w // 2, 0)).astype(F32)
        pooled = win / cnt - pg[POOL_HALO:POOL_HALO + tm]
        mixed.append(_mm(pooled, wg_ref[gi]))
    y = jnp.concatenate(mixed, axis=1) * ps_ref[...] * _silu(z)
    out = x + gt_ref[0] * _mm(y, wo_ref[...])
    o_ref[0] = _rms(out, fn_ref[...]) if final else out


def _pool_layer(x, shift, scale, gate, ln, wp, wz, wg, ps, wo, fnorm, per_batch_cond, final):
    b, l, d = x.shape
    tm = min(l, 256)
    hb = tm // POOL_HALO
    nb = l // POOL_HALO
    cidx = (lambda i, t: (i, 0, 0)) if per_batch_cond else (lambda i, t: (0, 0, 0))
    row = lambda i, t: (i, t, 0)
    return pl.pallas_call(
        functools.partial(_pool_kernel, final, l, tm),
        out_shape=jax.ShapeDtypeStruct((b, l, d), F32),
        grid=(b, l // tm),
        in_specs=[pl.BlockSpec((1, POOL_HALO, d), lambda i, t: (i, jnp.maximum(t * hb - 1, 0), 0)),
                  pl.BlockSpec((1, tm, d), row),
                  pl.BlockSpec((1, POOL_HALO, d), lambda i, t: (i, jnp.minimum((t + 1) * hb, nb - 1), 0)),
                  pl.BlockSpec((1, 1, d), cidx), pl.BlockSpec((1, 1, d), cidx), pl.BlockSpec((1, 1, d), cidx),
                  _resident((1, d)), _resident(wp.shape), _resident(wz.shape), _resident(wg.shape),
                  _resident((1, d)), _resident(wo.shape), _resident((1, d))],
        out_specs=pl.BlockSpec((1, tm, d), row),
        compiler_params=_params("parallel", "parallel"),
        name="pool_mixer",
    )(x, x, x, shift, scale, gate, ln, wp, wz, wg, ps, wo, fnorm)


def _rope_tables(n_tokens):
    rows = n_tokens // GRID_W
    row_pos = jnp.repeat(jnp.arange(rows), GRID_W).astype(F32)
    col_pos = jnp.tile(jnp.arange(GRID_W), rows).astype(F32)
    half = QK_ROPE // 2
    inv_freq = ROPE_BASE ** (-jnp.arange(0, half, 2, dtype=F32) / half)
    ar = row_pos[:, None] * inv_freq
    ac = col_pos[:, None] * inv_freq
    cos = jnp.concatenate([jnp.cos(ar), jnp.cos(ar), jnp.cos(ac), jnp.cos(ac)], axis=1)
    sin = jnp.concatenate([-jnp.sin(ar), jnp.sin(ar), -jnp.sin(ac), jnp.sin(ac)], axis=1)
    return jnp.concatenate([cos, cos], axis=1), jnp.concatenate([sin, sin], axis=1)


def _swap_perm():
    q = QK_ROPE // 4
    return jnp.array(list(range(q, 2 * q)) + list(range(q)) + list(range(3 * q, 4 * q)) + list(range(2 * q, 3 * q)))


def _even_weights(w_in, w_uq, w_ukv, w_out):
    d = w_in.shape[0]
    perm = _swap_perm()
    o = 0
    cq = w_in[:, o:o + D_CQ]; o += D_CQ
    ckv = w_in[:, o:o + D_CKV]; o += D_CKV
    kpe = w_in[:, o:o + QK_ROPE]; o += QK_ROPE
    qkv = w_in[:, o:o + QKV_W]; o += QKV_W
    ab = w_in[:, o:o + AB_W]; o += AB_W
    wz = w_in[:, o:]
    pad = lambda n: jnp.zeros((d, n), w_in.dtype)
    w = jnp.concatenate([cq, ckv, kpe, pad(LANES - QK_ROPE), kpe[:, perm], pad(LANES - QK_ROPE),
                         ab, pad(LANES - AB_W), qkv], axis=1).astype(BF16)
    uq = w_uq.reshape(D_CQ, A_HEADS, QK_NOPE + QK_ROPE)
    pe = uq[:, :, QK_NOPE:]
    wuq = jnp.concatenate([uq[:, :, :QK_NOPE].reshape(D_CQ, -1), pe.reshape(D_CQ, -1),
                           pe[:, :, perm].reshape(D_CQ, -1)], axis=1).astype(BF16)
    return w, wuq, w_ukv.astype(BF16), wz.astype(BF16), w_out.astype(BF16)


def kernel(x_prompt, x_sample, cache_ckv, cache_kpe, state_fwd, state_bwd, c, c_ctx, ln_e, mod_w_e, mod_b_e, w_in_e, q_norm_e, kv_norm_e, w_uq_e, w_ukv_e, conv_e, a_log_e, dt_bias_e, o_norm_e, w_out_e, ln_o, mod_w_o, mod_b_o, w_in_o, w_pool_o, pool_scale_o, w_out_o, final_norm):
    d = x_prompt.shape[-1]
    nb = c.shape[0]
    depth = ln_e.shape[0] + ln_o.shape[0]
    assert depth % 2 == 0, "the final norm is fused into the last (pooling) layer"
    cond = jnp.concatenate([c_ctx[None, :], c, jnp.zeros((COND_ROWS - 1 - nb, d), F32)], axis=0)
    mod = {0: _modulation(cond, mod_w_e, mod_b_e), 1: _modulation(cond, mod_w_o, mod_b_o)}
    rope = _rope_tables(x_sample.shape[1])
    fnorm = final_norm.reshape(1, d)
    xp, xs = x_prompt, x_sample
    ckv_out, kpe_out, sf_out, sb_out = [], [], [], []
    for layer in range(depth):
        i = layer // 2
        m = mod[layer % 2][i]
        sh_p, sc_p, gt_p = (m[0:1, j * d:(j + 1) * d].reshape(1, 1, d) for j in range(3))
        sh_s, sc_s, gt_s = (m[1:1 + nb, j * d:(j + 1) * d].reshape(nb, 1, d) for j in range(3))
        if layer % 2 == 0:
            ln = ln_e[i].reshape(1, d)
            w, wuq, wukv, wz, wo = _even_weights(w_in_e[i], w_uq_e[i], w_ukv_e[i], w_out_e[i])
            qg, kg = q_norm_e[i].reshape(1, D_CQ), kv_norm_e[i].reshape(1, D_CKV)
            gdn_w = (a_log_e[i], dt_bias_e[i], o_norm_e[i])
            in_w = (ln, w, conv_e[i], qg, kg, wuq, wukv)
            q, kv, kper, qkv, ab, ckv, kpe = _even_in(xp, sh_p, sc_p, *in_w, None, False, True)
            attn = _attention(q, kv, kper)
            g_o, s_f, s_b = _gdn(qkv, ab, *gdn_w, None)
            xp = _even_out(xp, sh_p, sc_p, gt_p, ln, attn, g_o, wz, wo, False)
            ckv_out.append(ckv); kpe_out.append(kpe); sf_out.append(s_f); sb_out.append(s_b)
            q, kv, kper, qkv, ab = _even_in(xs, sh_s, sc_s, *in_w, rope, True, False)
            attn = _attention(q, kv, kper, (_kv_up(cache_ckv[:, i], wukv), cache_kpe[:, i]))
            (g_o,) = _gdn(qkv, ab, *gdn_w, (state_fwd[:, i], state_bwd[:, i]))
            xs = _even_out(xs, sh_s, sc_s, gt_s, ln, attn, g_o, wz, wo, True)
        else:
            ln = ln_o[i].reshape(1, d)
            wp = w_in_o[i][:, :d].astype(BF16)
            wzo = w_in_o[i][:, d:].astype(BF16)
            wg = w_pool_o[i].astype(BF16)
            ps = pool_scale_o[i].reshape(1, d)
            wo = w_out_o[i].astype(BF16)
            final = layer == depth - 1
            xp = _pool_layer(xp, sh_p, sc_p, gt_p, ln, wp, wzo, wg, ps, wo, fnorm, False, final)
            xs = _pool_layer(xs, sh_s, sc_s, gt_s, ln, wp, wzo, wg, ps, wo, fnorm, True, final)
    return (xp, xs, jnp.stack(ckv_out, axis=1), jnp.stack(kpe_out, axis=1),
            jnp.stack(sf_out, axis=1), jnp.stack(sb_out, axis=1))
```

```python
import functools
import math

import jax
import jax.numpy as jnp
from jax import lax
from jax.experimental import pallas as pl
from jax.experimental.pallas import tpu as pltpu

F32 = jnp.float32
BF16 = jnp.bfloat16

EPS = 1e-6
GRID_W = 64
ROPE_BASE = 10000.0
A_HEADS = 8
QK_NOPE = 128
QK_ROPE = 64
V_HEAD = 128
D_CQ = 512
D_CKV = 512
B_HEADS = 8
DK = 128
DV = 128
CHUNK = 64
INV_BLOCK = 16
GDN_GROUP = 32
ATTN_BLOCK = 256
ATTN_TILE = 2048
POOL_WINDOWS = (2, 4, 8, 16)
POOL_HALO = 8
CONV_HALO = 8
QKV_W = B_HEADS * (2 * DK + DV)
AB_W = 4 * B_HEADS
LANES = 128
V7X_VMEM_BYTES = 64 * 1024 * 1024
VMEM_LIMIT = V7X_VMEM_BYTES - 8 * 1024 * 1024
COND_ROWS = 16

_C_CQ = 0
_C_CKV = D_CQ
_C_KPE = D_CQ + D_CKV
_C_KPE_SW = _C_KPE + LANES
_C_AB = _C_KPE_SW + LANES
_C_QKV = _C_AB + LANES
IN_W = _C_QKV + QKV_W


def _mm(a, b):
    return jnp.dot(a.astype(BF16), b.astype(BF16), preferred_element_type=F32)


def _mm_nt(a, b):
    return lax.dot_general(a.astype(BF16), b.astype(BF16), (((1,), (1,)), ((), ())),
                           preferred_element_type=F32)


def _mm_tn(a, b):
    return lax.dot_general(a.astype(BF16), b.astype(BF16), (((0,), (0,)), ((), ())),
                           preferred_element_type=F32)


def _sigmoid(x):
    return 1.0 / (1.0 + jnp.exp(-x))


def _silu(x):
    return x * _sigmoid(x)


def _softplus(x):
    return jnp.maximum(x, 0.0) + jnp.log(1.0 + jnp.exp(-jnp.abs(x)))


def _rms(x, g):
    return x * lax.rsqrt(jnp.mean(x * x, axis=-1, keepdims=True) + EPS) * g


def _adaln(x, ln, sc, sh):
    return _rms(x, ln) * (1.0 + sc) + sh


def _params(*sem):
    return pltpu.CompilerParams(dimension_semantics=sem, vmem_limit_bytes=VMEM_LIMIT)


def _resident(shape):
    nd = len(shape)
    return pl.BlockSpec(shape, lambda *_: (0,) * nd, pipeline_mode=pl.Buffered(1))


def _mod_kernel(c_ref, w_ref, b_ref, o_ref):
    o_ref[0] = _mm(_silu(c_ref[...]), w_ref[0]) + b_ref[0]


def _modulation(cond, w, b):
    n, d, e = w.shape
    tn = 1024
    return pl.pallas_call(
        _mod_kernel,
        out_shape=jax.ShapeDtypeStruct((n, COND_ROWS, e), F32),
        grid=(n, e // tn),
        in_specs=[pl.BlockSpec((COND_ROWS, d), lambda l, j: (0, 0)),
                  pl.BlockSpec((1, d, tn), lambda l, j: (l, 0, j)),
                  pl.BlockSpec((1, 1, tn), lambda l, j: (l, 0, j))],
        out_specs=pl.BlockSpec((1, COND_ROWS, tn), lambda l, j: (l, 0, j)),
        compiler_params=_params("parallel", "parallel"),
        name="modulation",
    )(cond, w, b.reshape(n, 1, e))


def _even_in_kernel(rope, emit_cache, l, tm, xp_ref, x_ref, xn_ref, sh_ref, sc_ref, ln_ref, w_ref, cw_ref,
                    qg_ref, kg_ref, wuq_ref, wukv_ref, *rest):
    if rope:
        cos_ref, sin_ref, *outs = rest
    else:
        outs = rest
    if emit_cache:
        q_ref, kv_ref, kper_ref, qkv_ref, ab_ref, ckv_ref, kpe_ref = outs
    else:
        q_ref, kv_ref, kper_ref, qkv_ref, ab_ref = outs
    t = pl.program_id(1)
    rows = tm + 2 * CONV_HALO
    xa = jnp.concatenate([xp_ref[0], x_ref[0], xn_ref[0]], axis=0)
    ha = _adaln(xa, ln_ref[...], sc_ref[0], sh_ref[0])
    hab = ha.astype(BF16)

    ri = lax.broadcasted_iota(jnp.int32, (rows, 1), 0) + (t * tm - CONV_HALO)
    inside = jnp.logical_and(ri >= 0, ri < l)
    gw = B_HEADS * DK
    for part, unit_scale in enumerate((DK ** -0.5, 1.0, None)):
        cols = slice(part * gw, (part + 1) * gw)
        pre = jnp.where(inside, _mm(hab, w_ref[:, _C_QKV + part * gw:_C_QKV + (part + 1) * gw]), 0.0)
        conv = (cw_ref[0:1, cols] * pltpu.roll(pre, 1, 0) + cw_ref[1:2, cols] * pre
                + cw_ref[2:3, cols] * pltpu.roll(pre, rows - 1, 0))
        act = _silu(conv[CONV_HALO:CONV_HALO + tm])
        if unit_scale is None:
            qkv_ref[0, :, cols] = act
        else:
            for g in range(B_HEADS):
                blk = act[:, g * DK:(g + 1) * DK]
                unit = lax.rsqrt(jnp.sum(blk * blk, axis=-1, keepdims=True) + EPS) * unit_scale
                qkv_ref[0, :, part * gw + g * DK:part * gw + (g + 1) * DK] = blk * unit

    p = _mm(ha[CONV_HALO:CONV_HALO + tm], w_ref[:, :_C_QKV])
    kpe = p[:, _C_KPE:_C_KPE + QK_ROPE]
    ckv = _rms(p[:, _C_CKV:_C_CKV + D_CKV], kg_ref[...])
    qq = _mm(_rms(p[:, _C_CQ:_C_CQ + D_CQ], qg_ref[...]), wuq_ref[...])
    n_nope = A_HEADS * QK_NOPE
    n_pe = A_HEADS * QK_ROPE
    qpe = qq[:, n_nope:n_nope + n_pe]
    if rope:
        cos = cos_ref[...]
        sin = sin_ref[...]
        kpe = kpe * cos[:, :QK_ROPE] + p[:, _C_KPE_SW:_C_KPE_SW + QK_ROPE] * sin[:, :QK_ROPE]
        reps = n_pe // LANES
        qpe = (qpe * jnp.concatenate([cos] * reps, axis=1)
               + qq[:, n_nope + n_pe:] * jnp.concatenate([sin] * reps, axis=1))
    for hd in range(A_HEADS):
        q_ref[0, hd, :, :QK_NOPE] = qq[:, hd * QK_NOPE:(hd + 1) * QK_NOPE].astype(BF16)
        q_ref[0, hd, :, QK_NOPE:] = qpe[:, hd * QK_ROPE:(hd + 1) * QK_ROPE].astype(BF16)
    kv_ref[0] = _mm(ckv, wukv_ref[...]).astype(BF16)
    kper_ref[0] = kpe.astype(BF16)
    ab_ref[0] = p[:, _C_AB:_C_AB + LANES]
    if emit_cache:
        ckv_ref[0] = ckv
        kpe_ref[0] = p[:, _C_KPE:_C_KPE + QK_ROPE]


def _even_in(x, shift, scale, ln, w, conv, qg, kg, wuq, wukv, rope_tabs, per_batch_cond, emit_cache):
    b, l, d = x.shape
    tm = min(l, 256)
    hb = tm // CONV_HALO
    nb = l // CONV_HALO
    cidx = (lambda i, t: (i, 0, 0)) if per_batch_cond else (lambda i, t: (0, 0, 0))
    in_specs = [pl.BlockSpec((1, CONV_HALO, d), lambda i, t: (i, jnp.maximum(t * hb - 1, 0), 0)),
                pl.BlockSpec((1, tm, d), lambda i, t: (i, t, 0)),
                pl.BlockSpec((1, CONV_HALO, d), lambda i, t: (i, jnp.minimum((t + 1) * hb, nb - 1), 0)),
                pl.BlockSpec((1, 1, d), cidx),
                pl.BlockSpec((1, 1, d), cidx),
                _resident((1, d)), _resident(w.shape), _resident(conv.shape), _resident((1, D_CQ)),
                _resident((1, D_CKV)), _resident(wuq.shape), _resident(wukv.shape)]
    args = [x, x, x, shift, scale, ln, w, conv, qg, kg, wuq, wukv]
    if rope_tabs is not None:
        in_specs += [pl.BlockSpec((tm, LANES), lambda i, t: (t, 0))] * 2
        args += list(rope_tabs)
    kvw = A_HEADS * (QK_NOPE + V_HEAD)
    out_shape = [jax.ShapeDtypeStruct((b, A_HEADS, l, QK_NOPE + QK_ROPE), BF16),
                 jax.ShapeDtypeStruct((b, l, kvw), BF16),
                 jax.ShapeDtypeStruct((b, l, QK_ROPE), BF16),
                 jax.ShapeDtypeStruct((b, l, QKV_W), F32),
                 jax.ShapeDtypeStruct((b, l, LANES), F32)]
    out_specs = [pl.BlockSpec((1, A_HEADS, tm, QK_NOPE + QK_ROPE), lambda i, t: (i, 0, t, 0)),
                 pl.BlockSpec((1, tm, kvw), lambda i, t: (i, t, 0)),
                 pl.BlockSpec((1, tm, QK_ROPE), lambda i, t: (i, t, 0)),
                 pl.BlockSpec((1, tm, QKV_W), lambda i, t: (i, t, 0)),
                 pl.BlockSpec((1, tm, LANES), lambda i, t: (i, t, 0))]
    if emit_cache:
        out_shape += [jax.ShapeDtypeStruct((b, l, D_CKV), F32), jax.ShapeDtypeStruct((b, l, QK_ROPE), F32)]
        out_specs += [pl.BlockSpec((1, tm, D_CKV), lambda i, t: (i, t, 0)),
                      pl.BlockSpec((1, tm, QK_ROPE), lambda i, t: (i, t, 0))]
    return pl.pallas_call(
        functools.partial(_even_in_kernel, rope_tabs is not None, emit_cache, l, tm),
        out_shape=out_shape, grid=(b, l // tm), in_specs=in_specs, out_specs=out_specs,
        compiler_params=_params("parallel", "parallel"),
        name="even_in",
    )(*args)


def _kv_up_kernel(c_ref, w_ref, o_ref):
    o_ref[0] = _mm(c_ref[0], w_ref[...]).astype(BF16)


def _kv_up(ckv, wukv):
    b, p, c = ckv.shape
    e = wukv.shape[1]
    return pl.pallas_call(
        _kv_up_kernel,
        out_shape=jax.ShapeDtypeStruct((b, p, e), BF16),
        grid=(b,),
        in_specs=[pl.BlockSpec((1, p, c), lambda i: (i, 0, 0)), _resident(wukv.shape)],
        out_specs=pl.BlockSpec((1, p, e), lambda i: (i, 0, 0)),
        compiler_params=_params("parallel"),
        name="kv_up",
    )(ckv, wukv)


def _attn_kernel(n_ctx, hb, th, *refs):
    if n_ctx:
        q_ref, kvc_ref, pec_ref, kvl_ref, pel_ref, o_ref, kcat, vcat = refs
    else:
        q_ref, kvl_ref, pel_ref, o_ref, kcat, vcat = refs
    hw = QK_NOPE + V_HEAD

    @pl.when(pl.program_id(2) == 0)
    def _():
        for hd in range(hb):
            if n_ctx:
                kcat[hd, :n_ctx, :QK_NOPE] = kvc_ref[0, :, hd * hw:hd * hw + QK_NOPE]
                kcat[hd, :n_ctx, QK_NOPE:] = pec_ref[0].astype(BF16)
                vcat[hd, :n_ctx, :] = kvc_ref[0, :, hd * hw + QK_NOPE:(hd + 1) * hw]
            kcat[hd, n_ctx:, :QK_NOPE] = kvl_ref[0, :, hd * hw:hd * hw + QK_NOPE]
            kcat[hd, n_ctx:, QK_NOPE:] = pel_ref[0]
            vcat[hd, n_ctx:, :] = kvl_ref[0, :, hd * hw + QK_NOPE:(hd + 1) * hw]

    c = (QK_NOPE + QK_ROPE) ** -0.5 * math.log2(math.e)
    blocks = [(hd, j) for hd in range(hb) for j in range(q_ref.shape[2] // th)]

    def scores(hd, j):
        return _mm_nt(q_ref[0, hd, j * th:(j + 1) * th, :], kcat[hd])

    ahead = 2
    pending = [scores(*blk) for blk in blocks[:ahead]]
    for i, (hd, j) in enumerate(blocks):
        s = pending.pop(0)
        e = jnp.exp2((s - jnp.max(s, axis=-1, keepdims=True)) * c)
        o = _mm(e, vcat[hd])
        if i + ahead < len(blocks):
            pending.append(scores(*blocks[i + ahead]))
        o_ref[0, j * th:(j + 1) * th, hd * V_HEAD:(hd + 1) * V_HEAD] = o * (1.0 / jnp.sum(e, axis=-1, keepdims=True))


def _attention(q, kv, kpe, ctx=None):
    b, hh, l, dq = q.shape
    th = min(l, ATTN_BLOCK)
    tq = min(l, ATTN_TILE)
    hb = min(hh, ATTN_TILE // tq)
    hw = QK_NOPE + V_HEAD
    n_ctx = 0 if ctx is None else ctx[0].shape[1]
    lk = n_ctx + l
    in_specs = [pl.BlockSpec((1, hb, tq, dq), lambda i, h, t: (i, h, t, 0))]
    args = [q]
    if ctx is not None:
        in_specs += [pl.BlockSpec((1, n_ctx, hb * hw), lambda i, h, t: (i, 0, h)),
                     pl.BlockSpec((1, n_ctx, QK_ROPE), lambda i, h, t: (i, 0, 0))]
        args += list(ctx)
    in_specs += [pl.BlockSpec((1, l, hb * hw), lambda i, h, t: (i, 0, h)),
                 pl.BlockSpec((1, l, QK_ROPE), lambda i, h, t: (i, 0, 0))]
    args += [kv, kpe]
    return pl.pallas_call(
        functools.partial(_attn_kernel, n_ctx, hb, th),
        out_shape=jax.ShapeDtypeStruct((b, l, hh * V_HEAD), F32),
        grid=(b, hh // hb, l // tq),
        in_specs=in_specs,
        out_specs=pl.BlockSpec((1, tq, hb * V_HEAD), lambda i, h, t: (i, t, h)),
        scratch_shapes=[pltpu.VMEM((hb, lk, dq), BF16), pltpu.VMEM((hb, lk, V_HEAD), BF16)],
        compiler_params=_params("parallel", "parallel", "arbitrary"),
        name="attention",
    )(*args)


def _unit_triangular_inverses(mats, rowi, coli, second):
    def diag(y):
        return jnp.concatenate([jnp.where(second, 0.0, y), jnp.where(second, y, 0.0)], axis=0)

    eye = (rowi == coli).astype(F32)
    same = (rowi // INV_BLOCK) == (coli // INV_BLOCK)
    ps = [jnp.where(same, a, 0.0) for a in mats]
    ts = [eye - p for p in ps]
    for _ in range(int(math.log2(INV_BLOCK)) - 1):
        ps = [_mm(p, diag(p)) for p in ps]
        ts = [t + _mm(t, diag(p)) for t, p in zip(ts, ps)]
    size = INV_BLOCK
    while size < CHUNK:
        wider = (rowi // (2 * size)) == (coli // (2 * size))
        pick = jnp.logical_and(wider, jnp.logical_not(same))
        cts = [_mm(jnp.where(pick, a, 0.0), diag(t)) for a, t in zip(mats, ts)]
        ts = [t - _mm(t, diag(ct)) for t, ct in zip(ts, cts)]
        same = wider
        size *= 2
    return ts


def _two_blocks(top, bottom):
    zeros = jnp.zeros_like(top)
    return jnp.concatenate([jnp.concatenate([top, zeros], axis=1), jnp.concatenate([zeros, bottom], axis=1)], axis=0)


def _gdn_kernel(has_state, l, hb, group, qr_ref, kr_ref, vr_ref, abc_ref, abr_ref, alog_ref, dtb_ref, on_ref, *rest):
    if has_state:
        s0f_ref, s0b_ref, o_ref = rest[:3]
        scr = rest[3:]
        sf_ref = sb_ref = None
    else:
        o_ref, sf_ref, sb_ref = rest[:3]
        scr = rest[3:]
    (km_f, nm_f, qp_f, o_f, dc_f, km_b, nm_b, qp_b, o_b, dc_b) = scr
    per_dir = ((km_f, nm_f, qp_f, o_f, dc_f), (km_b, nm_b, qp_b, o_b, dc_b))
    n = l // CHUNK
    rb = min(l, 512)
    heads = range(hb)

    neg_a = [[-jnp.exp(alog_ref[hd, d]) for d in (0, 1)] for hd in heads]
    dt = [[dtb_ref[hd, d] for d in (0, 1)] for hd in heads]
    rowi = lax.broadcasted_iota(jnp.int32, (CHUNK, 2 * CHUNK), 0)
    lane = lax.broadcasted_iota(jnp.int32, (CHUNK, 2 * CHUNK), 1)
    bwd = lane >= CHUNK
    coli = jnp.where(bwd, lane - CHUNK, lane)
    ahead = jnp.where(bwd, coli - rowi, rowi - coli)
    tri = ahead >= 0
    tri_t = ahead <= 0

    def load_chunk(hd, c):
        r0 = pl.multiple_of(c * CHUNK, CHUNK)
        lanes = slice(hd * DK, (hd + 1) * DK)
        return (qr_ref[0, pl.ds(r0, CHUNK), lanes], kr_ref[0, pl.ds(r0, CHUNK), lanes],
                vr_ref[0, pl.ds(r0, CHUNK), lanes],
                abc_ref[0, hd, pl.ds(r0, CHUNK), :],
                [abr_ref[0, hd, d, pl.ds(c, 1), :] for d in (0, 1)])

    def gates(hd, k, v, abc, abr):
        g_col = [neg_a[hd][d] * _softplus(abc[:, d:d + 1] + dt[hd][d]) for d in (0, 1)]
        g_row = jnp.concatenate([neg_a[hd][d] * _softplus(abr[d] + dt[hd][d]) for d in (0, 1)], axis=1)
        beta = [_sigmoid(abc[:, 2 + d:3 + d]) for d in (0, 1)]
        terms = jnp.where(tri, jnp.broadcast_to(g_row, (CHUNK, 2 * CHUNK)), 0.0)
        gc = [jnp.sum(jnp.where(bwd, 0.0, terms), axis=1, keepdims=True),
              jnp.sum(jnp.where(bwd, terms, 0.0), axis=1, keepdims=True)]
        gr = jnp.sum(jnp.where(tri_t, jnp.where(bwd, g_col[1], g_col[0]), 0.0), axis=0, keepdims=True)
        gamma = jnp.exp(jnp.where(tri, jnp.where(bwd, gc[1], gc[0]) - gr, -jnp.inf))
        kb = [k * beta[d] for d in (0, 1)]
        e_gc = [jnp.exp(gc[d]) for d in (0, 1)]
        g_end = [gc[0][CHUNK - 1:CHUNK, :], gc[1][0:1, :]]
        return dict(kb=kb, gamma=gamma, strict=jnp.where(rowi == coli, 0.0, gamma), e_gc=e_gc,
                    rhs=[jnp.concatenate([v * beta[d], kb[d] * e_gc[d]], axis=1) for d in (0, 1)],
                    kd=[(k * jnp.exp(g_end[d] - gc[d])).astype(BF16) for d in (0, 1)],
                    dec=[jnp.broadcast_to(jnp.exp(g_end[d]), (8, DV)) for d in (0, 1)])

    def chunk_group(i, carry):
        where = [(hd, i * group + j) for hd in heads for j in range(group)]
        loaded = [load_chunk(hd, c) for hd, c in where]
        gs = [gates(hd, k, v, abc, abr) for (hd, _), (q, k, v, abc, abr) in zip(where, loaded)]
        stacked = [_mm_nt(jnp.concatenate([jnp.concatenate([q, q], axis=1),
                                           jnp.concatenate(g["kb"], axis=1)], axis=0), _two_blocks(k, k))
                   for (q, k, *_), g in zip(loaded, gs)]
        invs = _unit_triangular_inverses([st[CHUNK:] * g["strict"] for st, g in zip(stacked, gs)], rowi, coli, bwd)
        sols = [_mm(t, _two_blocks(*g["rhs"])).astype(BF16) for t, g in zip(invs, gs)]
        sold = [[s[:, :2 * DV], s[:, 2 * DV:]] for s in sols]
        in_sol = [_mm(st[:CHUNK] * g["gamma"], _two_blocks(*sd)) for st, g, sd in zip(stacked, gs, sold)]
        kd_sol = [[_mm_tn(g["kd"][d], sd[d]) for d in (0, 1)] for g, sd in zip(gs, sold)]
        for (hd, c), (q, *_), g, is_, ks_ in zip(where, loaded, gs, in_sol, kd_sol):
            r0 = pl.multiple_of(c * CHUNK, CHUNK)
            m0 = pl.multiple_of(c * DK, DK)
            for d in (0, 1):
                km_s, nm_s, qp_s, o_s, dc_s = per_dir[d]
                nm_s[hd, pl.ds(m0, DK), :] = ks_[d][:, :DV]
                km_s[hd, pl.ds(m0, DK), :] = ks_[d][:, DV:].astype(BF16)
                o_s[hd, pl.ds(r0, CHUNK), :] = is_[:, 2 * d * DV:(2 * d + 1) * DV]
                qp_s[hd, pl.ds(r0, CHUNK), :] = (q * g["e_gc"][d] - is_[:, (2 * d + 1) * DV:(2 * d + 2) * DV]
                                                 ).astype(BF16)
                dc_s[hd, pl.ds(pl.multiple_of(c * 8, 8), 8), :] = g["dec"][d]
        return carry

    lax.fori_loop(0, n // group, chunk_group, 0)

    def load_step(hd, d, c):
        km_s, nm_s, qp_s, o_s, dc_s = per_dir[d]
        r0 = pl.multiple_of(c * CHUNK, CHUNK)
        m0 = pl.multiple_of(c * DK, DK)
        return (km_s[hd, pl.ds(m0, DK), :], nm_s[hd, pl.ds(m0, DK), :], qp_s[hd, pl.ds(r0, CHUNK), :],
                o_s[hd, pl.ds(r0, CHUNK), :], dc_s[hd, pl.ds(pl.multiple_of(c * 8, 8), 8), :][0:1, :])

    chains = [(hd, d) for hd in heads for d in (0, 1)]

    def scan_step(c, states):
        cs = [c if d == 0 else n - 1 - c for _, d in chains]
        loaded = [load_step(hd, d, cc) for (hd, d), cc in zip(chains, cs)]
        sbs = [s.astype(BF16) for s in states]
        new = [s * dec + nm - _mm(km, sb) for (km, nm, qp, o_loc, dec), s, sb in zip(loaded, states, sbs)]
        outs = [o_loc + _mm(qp, sb) for (km, nm, qp, o_loc, dec), sb in zip(loaded, sbs)]
        for (hd, d), cc, out in zip(chains, cs, outs):
            per_dir[d][3][hd, pl.ds(pl.multiple_of(cc * CHUNK, CHUNK), CHUNK), :] = out
        return tuple(new)

    if has_state:
        init = tuple((s0f_ref, s0b_ref)[d][0, hd] for hd, d in chains)
    else:
        init = tuple(jnp.zeros((DK, DV), F32) for _ in chains)
    final = lax.fori_loop(0, n, scan_step, init)
    if not has_state:
        for (hd, d), s in zip(chains, final):
            (sf_ref, sb_ref)[d][0, hd] = s

    def out_block(i, carry):
        r0 = pl.multiple_of(i * rb, rb)
        for hd in heads:
            o_ref[0, pl.ds(r0, rb), hd * DV:(hd + 1) * DV] = _rms(
                o_f[hd, pl.ds(r0, rb), :] + o_b[hd, pl.ds(r0, rb), :], on_ref[...])
        return carry

    lax.fori_loop(0, l // rb, out_block, 0)


def _gdn(qkv, ab, a_log, dt_bias, o_norm, states):
    b, l, _ = qkv.shape
    n = l // CHUNK
    hh = B_HEADS
    ab4 = jnp.transpose(ab[:, :, :AB_W].reshape(b, l, 4, hh), (0, 3, 1, 2))
    abr = jnp.transpose(ab[:, :, :AB_W].reshape(b, n, CHUNK, 4, hh), (0, 4, 3, 1, 2))
    alog = jnp.transpose(a_log).reshape(hh, 2, 1, 1)
    dtb = jnp.transpose(dt_bias).reshape(hh, 2, 1, 1)
    has_state = states is not None
    hb = max(1, min(hh, GDN_GROUP // n))
    hg = hh // hb
    in_specs = [pl.BlockSpec((1, l, hb * DK), lambda i, h: (i, 0, h)),
                pl.BlockSpec((1, l, hb * DK), lambda i, h: (i, 0, hg + h)),
                pl.BlockSpec((1, l, hb * DV), lambda i, h: (i, 0, 2 * hg + h)),
                pl.BlockSpec((1, hb, l, 4), lambda i, h: (i, h, 0, 0)),
                pl.BlockSpec((1, hb, 4, n, CHUNK), lambda i, h: (i, h, 0, 0, 0)),
                pl.BlockSpec((hb, 2, 1, 1), lambda i, h: (h, 0, 0, 0)),
                pl.BlockSpec((hb, 2, 1, 1), lambda i, h: (h, 0, 0, 0)),
                pl.BlockSpec((1, DV), lambda i, h: (0, 0))]
    args = [qkv, qkv, qkv, ab4, abr, alog, dtb, o_norm.reshape(1, DV)]
    st_spec = pl.BlockSpec((1, hb, DK, DV), lambda i, h: (i, h, 0, 0))
    out_shape = [jax.ShapeDtypeStruct((b, l, hh * DV), F32)]
    out_specs = [pl.BlockSpec((1, l, hb * DV), lambda i, h: (i, 0, h))]
    if has_state:
        in_specs += [st_spec, st_spec]
        args += list(states)
    else:
        out_shape += [jax.ShapeDtypeStruct((b, hh, DK, DV), F32)] * 2
        out_specs += [st_spec, st_spec]
    per_dir = [pltpu.VMEM((hb, n * DK, DV), BF16), pltpu.VMEM((hb, n * DK, DV), F32),
               pltpu.VMEM((hb, l, DK), BF16), pltpu.VMEM((hb, l, DV), F32), pltpu.VMEM((hb, n * 8, DV), F32)]
    scratch = per_dir + per_dir
    group = math.gcd(n, max(1, GDN_GROUP // hb))
    return pl.pallas_call(
        functools.partial(_gdn_kernel, has_state, l, hb, group),
        out_shape=out_shape, grid=(b, hg), in_specs=in_specs, out_specs=out_specs,
        scratch_shapes=scratch,
        compiler_params=_params("parallel", "parallel"),
        name="gated_deltanet",
    )(*args)


def _even_out_kernel(x_ref, sh_ref, sc_ref, gt_ref, ln_ref, at_ref, gd_ref, wz_ref, wo_ref, o_ref):
    x = x_ref[0]
    z = _mm(_adaln(x, ln_ref[...], sc_ref[0], sh_ref[0]), wz_ref[...])
    y = jnp.concatenate([at_ref[0], gd_ref[0]], axis=1) * _silu(z)
    o_ref[0] = x + gt_ref[0] * _mm(y, wo_ref[...])


def _even_out(x, shift, scale, gate, ln, attn, gdn, wz, wo, per_batch_cond):
    b, l, d = x.shape
    tm = min(l, 256)
    cidx = (lambda i, t: (i, 0, 0)) if per_batch_cond else (lambda i, t: (0, 0, 0))
    row = lambda i, t: (i, t, 0)
    return pl.pallas_call(
        _even_out_kernel,
        out_shape=jax.ShapeDtypeStruct((b, l, d), F32),
        grid=(b, l // tm),
        in_specs=[pl.BlockSpec((1, tm, d), row), pl.BlockSpec((1, 1, d), cidx), pl.BlockSpec((1, 1, d), cidx),
                  pl.BlockSpec((1, 1, d), cidx), _resident((1, d)),
                  pl.BlockSpec((1, tm, attn.shape[2]), row), pl.BlockSpec((1, tm, gdn.shape[2]), row),
                  _resident(wz.shape), _resident(wo.shape)],
        out_specs=pl.BlockSpec((1, tm, d), row),
        compiler_params=_params("parallel", "parallel"),
        name="even_out",
    )(x, shift, scale, gate, ln, attn, gdn, wz, wo)


def _pool_kernel(final, l, tm, xp_ref, x_ref, xn_ref, sh_ref, sc_ref, gt_ref, ln_ref, wp_ref, wz_ref,
                 wg_ref, ps_ref, wo_ref, fn_ref, o_ref):
    t = pl.program_id(1)
    x = x_ref[0]
    rows = tm + 2 * POOL_HALO
    xa = jnp.concatenate([xp_ref[0], x, xn_ref[0]], axis=0)
    h = _adaln(xa, ln_ref[...], sc_ref[0], sh_ref[0])
    ri = lax.broadcasted_iota(jnp.int32, (rows, 1), 0) + (t * tm - POOL_HALO)
    inside = jnp.logical_and(ri >= 0, ri < l)
    pin = jnp.where(inside, _mm(h, wp_ref[...]), 0.0)
    z = _mm(h[POOL_HALO:POOL_HALO + tm], wz_ref[...])
    pos = lax.broadcasted_iota(jnp.int32, (tm, 1), 0) + t * tm
    gw = pin.shape[1] // len(POOL_WINDOWS)
    mixed = []
    for gi, w in enumerate(POOL_WINDOWS):
        pg = pin[:, gi * gw:(gi + 1) * gw]
        acc = pg
        span = 1
        while span < w:
            acc = acc + pltpu.roll(acc, rows - span, 0)
            span *= 2
        first = POOL_HALO - w // 2
        win = pltpu.roll(acc, rows - first, 0)[:tm] if first else acc[:tm]
        cnt = (jnp.minimum(pos + (w - w // 2), l) - jnp.maximum(pos - w // 2, 0)).astype(F32)
        pooled = win / cnt - pg[POOL_HALO:POOL_HALO + tm]
        mixed.append(_mm(pooled, wg_ref[gi]))
    y = jnp.concatenate(mixed, axis=1) * ps_ref[...] * _silu(z)
    out = x + gt_ref[0] * _mm(y, wo_ref[...])
    o_ref[0] = _rms(out, fn_ref[...]) if final else out


def _pool_layer(x, shift, scale, gate, ln, wp, wz, wg, ps, wo, fnorm, per_batch_cond, final):
    b, l, d = x.shape
    tm = min(l, 256)
    hb = tm // POOL_HALO
    nb = l // POOL_HALO
    cidx = (lambda i, t: (i, 0, 0)) if per_batch_cond else (lambda i, t: (0, 0, 0))
    row = lambda i, t: (i, t, 0)
    return pl.pallas_call(
        functools.partial(_pool_kernel, final, l, tm),
        out_shape=jax.ShapeDtypeStruct((b, l, d), F32),
        grid=(b, l // tm),
        in_specs=[pl.BlockSpec((1, POOL_HALO, d), lambda i, t: (i, jnp.maximum(t * hb - 1, 0), 0)),
                  pl.BlockSpec((1, tm, d), row),
                  pl.BlockSpec((1, POOL_HALO, d), lambda i, t: (i, jnp.minimum((t + 1) * hb, nb - 1), 0)),
                  pl.BlockSpec((1, 1, d), cidx), pl.BlockSpec((1, 1, d), cidx), pl.BlockSpec((1, 1, d), cidx),
                  _resident((1, d)), _resident(wp.shape), _resident(wz.shape), _resident(wg.shape),
                  _resident((1, d)), _resident(wo.shape), _resident((1, d))],
        out_specs=pl.BlockSpec((1, tm, d), row),
        compiler_params=_params("parallel", "parallel"),
        name="pool_mixer",
    )(x, x, x, shift, scale, gate, ln, wp, wz, wg, ps, wo, fnorm)


def _rope_tables(n_tokens):
    rows = n_tokens // GRID_W
    row_pos = jnp.repeat(jnp.arange(rows), GRID_W).astype(F32)
    col_pos = jnp.tile(jnp.arange(GRID_W), rows).astype(F32)
    half = QK_ROPE // 2
    inv_freq = ROPE_BASE ** (-jnp.arange(0, half, 2, dtype=F32) / half)
    ar = row_pos[:, None] * inv_freq
    ac = col_pos[:, None] * inv_freq
    cos = jnp.concatenate([jnp.cos(ar), jnp.cos(ar), jnp.cos(ac), jnp.cos(ac)], axis=1)
    sin = jnp.concatenate([-jnp.sin(ar), jnp.sin(ar), -jnp.sin(ac), jnp.sin(ac)], axis=1)
    return jnp.concatenate([cos, cos], axis=1), jnp.concatenate([sin, sin], axis=1)


def _swap_perm():
    q = QK_ROPE // 4
    return jnp.array(list(range(q, 2 * q)) + list(range(q)) + list(range(3 * q, 4 * q)) + list(range(2 * q, 3 * q)))


def _even_weights(w_in, w_uq, w_ukv, w_out):
    d = w_in.shape[0]
    perm = _swap_perm()
    o = 0
    cq = w_in[:, o:o + D_CQ]; o += D_CQ
    ckv = w_in[:, o:o + D_CKV]; o += D_CKV
    kpe = w_in[:, o:o + QK_ROPE]; o += QK_ROPE
    qkv = w_in[:, o:o + QKV_W]; o += QKV_W
    ab = w_in[:, o:o + AB_W]; o += AB_W
    wz = w_in[:, o:]
    pad = lambda n: jnp.zeros((d, n), w_in.dtype)
    w = jnp.concatenate([cq, ckv, kpe, pad(LANES - QK_ROPE), kpe[:, perm], pad(LANES - QK_ROPE),
                         ab, pad(LANES - AB_W), qkv], axis=1).astype(BF16)
    uq = w_uq.reshape(D_CQ, A_HEADS, QK_NOPE + QK_ROPE)
    pe = uq[:, :, QK_NOPE:]
    wuq = jnp.concatenate([uq[:, :, :QK_NOPE].reshape(D_CQ, -1), pe.reshape(D_CQ, -1),
                           pe[:, :, perm].reshape(D_CQ, -1)], axis=1).astype(BF16)
    return w, wuq, w_ukv.astype(BF16), wz.astype(BF16), w_out.astype(BF16)


def kernel(x_prompt, x_sample, cache_ckv, cache_kpe, state_fwd, state_bwd, c, c_ctx, ln_e, mod_w_e, mod_b_e, w_in_e, q_norm_e, kv_norm_e, w_uq_e, w_ukv_e, conv_e, a_log_e, dt_bias_e, o_norm_e, w_out_e, ln_o, mod_w_o, mod_b_o, w_in_o, w_pool_o, pool_scale_o, w_out_o, final_norm):
    d = x_prompt.shape[-1]
    nb = c.shape[0]
    depth = ln_e.shape[0] + ln_o.shape[0]
    assert depth % 2 == 0, "the final norm is fused into the last (pooling) layer"
    cond = jnp.concatenate([c_ctx[None, :], c, jnp.zeros((COND_ROWS - 1 - nb, d), F32)], axis=0)
    mod = {0: _modulation(cond, mod_w_e, mod_b_e), 1: _modulation(cond, mod_w_o, mod_b_o)}
    rope = _rope_tables(x_sample.shape[1])
    fnorm = final_norm.reshape(1, d)
    xp, xs = x_prompt, x_sample
    ckv_out, kpe_out, sf_out, sb_out = [], [], [], []
    for layer in range(depth):
        i = layer // 2
        m = mod[layer % 2][i]
        sh_p, sc_p, gt_p = (m[0:1, j * d:(j + 1) * d].reshape(1, 1, d) for j in range(3))
        sh_s, sc_s, gt_s = (m[1:1 + nb, j * d:(j + 1) * d].reshape(nb, 1, d) for j in range(3))
        if layer % 2 == 0:
            ln = ln_e[i].reshape(1, d)
            w, wuq, wukv, wz, wo = _even_weights(w_in_e[i], w_uq_e[i], w_ukv_e[i], w_out_e[i])
            qg, kg = q_norm_e[i].reshape(1, D_CQ), kv_norm_e[i].reshape(1, D_CKV)
            gdn_w = (a_log_e[i], dt_bias_e[i], o_norm_e[i])
            in_w = (ln, w, conv_e[i], qg, kg, wuq, wukv)
            q, kv, kper, qkv, ab, ckv, kpe = _even_in(xp, sh_p, sc_p, *in_w, None, False, True)
            attn = _attention(q, kv, kper)
            g_o, s_f, s_b = _gdn(qkv, ab, *gdn_w, None)
            xp = _even_out(xp, sh_p, sc_p, gt_p, ln, attn, g_o, wz, wo, False)
            ckv_out.append(ckv); kpe_out.append(kpe); sf_out.append(s_f); sb_out.append(s_b)
            q, kv, kper, qkv, ab = _even_in(xs, sh_s, sc_s, *in_w, rope, True, False)
            attn = _attention(q, kv, kper, (_kv_up(cache_ckv[:, i], wukv), cache_kpe[:, i]))
            (g_o,) = _gdn(qkv, ab, *gdn_w, (state_fwd[:, i], state_bwd[:, i]))
            xs = _even_out(xs, sh_s, sc_s, gt_s, ln, attn, g_o, wz, wo, True)
        else:
            ln = ln_o[i].reshape(1, d)
            wp = w_in_o[i][:, :d].astype(BF16)
            wzo = w_in_o[i][:, d:].astype(BF16)
            wg = w_pool_o[i].astype(BF16)
            ps = pool_scale_o[i].reshape(1, d)
            wo = w_out_o[i].astype(BF16)
            final = layer == depth - 1
            xp = _pool_layer(xp, sh_p, sc_p, gt_p, ln, wp, wzo, wg, ps, wo, fnorm, False, final)
            xs = _pool_layer(xs, sh_s, sc_s, gt_s, ln, wp, wzo, wg, ps, wo, fnorm, True, final)
    return (xp, xs, jnp.stack(ckv_out, axis=1), jnp.stack(kpe_out, axis=1),
            jnp.stack(sf_out, axis=1), jnp.stack(sb_out, axis=1))
```

```python
import functools
import math

import jax
import jax.numpy as jnp
from jax import lax
from jax.experimental import pallas as pl
from jax.experimental.pallas import tpu as pltpu

F32 = jnp.float32
BF16 = jnp.bfloat16

EPS = 1e-6
GRID_W = 64
ROPE_BASE = 10000.0
A_HEADS = 8
QK_NOPE = 128
QK_ROPE = 64
V_HEAD = 128
D_CQ = 512
D_CKV = 512
B_HEADS = 8
DK = 128
DV = 128
CHUNK = 64
INV_BLOCK = 16
GDN_GROUP = 16
ATTN_BLOCK = 256
ATTN_TILE = 2048
POOL_WINDOWS = (2, 4, 8, 16)
POOL_HALO = 8
CONV_HALO = 8
QKV_W = B_HEADS * (2 * DK + DV)
AB_W = 4 * B_HEADS
LANES = 128
V7X_VMEM_BYTES = 64 * 1024 * 1024
VMEM_LIMIT = V7X_VMEM_BYTES - 8 * 1024 * 1024
COND_ROWS = 16

_C_CQ = 0
_C_CKV = D_CQ
_C_KPE = D_CQ + D_CKV
_C_KPE_SW = _C_KPE + LANES
_C_AB = _C_KPE_SW + LANES
_C_QKV = _C_AB + LANES
IN_W = _C_QKV + QKV_W


def _mm(a, b):
    return jnp.dot(a.astype(BF16), b.astype(BF16), preferred_element_type=F32)


def _mm_nt(a, b):
    return lax.dot_general(a.astype(BF16), b.astype(BF16), (((1,), (1,)), ((), ())),
                           preferred_element_type=F32)


def _mm_tn(a, b):
    return lax.dot_general(a.astype(BF16), b.astype(BF16), (((0,), (0,)), ((), ())),
                           preferred_element_type=F32)


def _sigmoid(x):
    return 1.0 / (1.0 + jnp.exp(-x))


def _silu(x):
    return x * _sigmoid(x)


def _softplus(x):
    return jnp.maximum(x, 0.0) + jnp.log(1.0 + jnp.exp(-jnp.abs(x)))


def _rms(x, g):
    return x * lax.rsqrt(jnp.mean(x * x, axis=-1, keepdims=True) + EPS) * g


def _adaln(x, ln, sc, sh):
    return _rms(x, ln) * (1.0 + sc) + sh


def _params(*sem):
    return pltpu.CompilerParams(dimension_semantics=sem, vmem_limit_bytes=VMEM_LIMIT)


def _resident(shape):
    nd = len(shape)
    return pl.BlockSpec(shape, lambda *_: (0,) * nd, pipeline_mode=pl.Buffered(1))


def _mod_kernel(c_ref, w_ref, b_ref, o_ref):
    o_ref[0] = _mm(_silu(c_ref[...]), w_ref[0]) + b_ref[0]


def _modulation(cond, w, b):
    n, d, e = w.shape
    tn = 1024
    return pl.pallas_call(
        _mod_kernel,
        out_shape=jax.ShapeDtypeStruct((n, COND_ROWS, e), F32),
        grid=(n, e // tn),
        in_specs=[pl.BlockSpec((COND_ROWS, d), lambda l, j: (0, 0)),
                  pl.BlockSpec((1, d, tn), lambda l, j: (l, 0, j)),
                  pl.BlockSpec((1, 1, tn), lambda l, j: (l, 0, j))],
        out_specs=pl.BlockSpec((1, COND_ROWS, tn), lambda l, j: (l, 0, j)),
        compiler_params=_params("parallel", "parallel"),
        name="modulation",
    )(cond, w, b.reshape(n, 1, e))


def _even_in_kernel(rope, emit_cache, l, tm, xp_ref, x_ref, xn_ref, sh_ref, sc_ref, ln_ref, w_ref, cw_ref,
                    qg_ref, kg_ref, wuq_ref, wukv_ref, *rest):
    if rope:
        cos_ref, sin_ref, *outs = rest
    else:
        outs = rest
    if emit_cache:
        q_ref, kv_ref, kper_ref, qkv_ref, ab_ref, ckv_ref, kpe_ref = outs
    else:
        q_ref, kv_ref, kper_ref, qkv_ref, ab_ref = outs
    t = pl.program_id(1)
    rows = tm + 2 * CONV_HALO
    xa = jnp.concatenate([xp_ref[0], x_ref[0], xn_ref[0]], axis=0)
    ha = _adaln(xa, ln_ref[...], sc_ref[0], sh_ref[0])
    hab = ha.astype(BF16)

    ri = lax.broadcasted_iota(jnp.int32, (rows, 1), 0) + (t * tm - CONV_HALO)
    inside = jnp.logical_and(ri >= 0, ri < l)
    gw = B_HEADS * DK
    for part, unit_scale in enumerate((DK ** -0.5, 1.0, None)):
        cols = slice(part * gw, (part + 1) * gw)
        pre = jnp.where(inside, _mm(hab, w_ref[:, _C_QKV + part * gw:_C_QKV + (part + 1) * gw]), 0.0)
        conv = (cw_ref[0:1, cols] * pltpu.roll(pre, 1, 0) + cw_ref[1:2, cols] * pre
                + cw_ref[2:3, cols] * pltpu.roll(pre, rows - 1, 0))
        act = _silu(conv[CONV_HALO:CONV_HALO + tm])
        if unit_scale is None:
            qkv_ref[0, :, cols] = act
        else:
            for g in range(B_HEADS):
                blk = act[:, g * DK:(g + 1) * DK]
                unit = lax.rsqrt(jnp.sum(blk * blk, axis=-1, keepdims=True) + EPS) * unit_scale
                qkv_ref[0, :, part * gw + g * DK:part * gw + (g + 1) * DK] = blk * unit

    p = _mm(ha[CONV_HALO:CONV_HALO + tm], w_ref[:, :_C_QKV])
    kpe = p[:, _C_KPE:_C_KPE + QK_ROPE]
    ckv = _rms(p[:, _C_CKV:_C_CKV + D_CKV], kg_ref[...])
    qq = _mm(_rms(p[:, _C_CQ:_C_CQ + D_CQ], qg_ref[...]), wuq_ref[...])
    n_nope = A_HEADS * QK_NOPE
    n_pe = A_HEADS * QK_ROPE
    qpe = qq[:, n_nope:n_nope + n_pe]
    if rope:
        cos = cos_ref[...]
        sin = sin_ref[...]
        kpe = kpe * cos[:, :QK_ROPE] + p[:, _C_KPE_SW:_C_KPE_SW + QK_ROPE] * sin[:, :QK_ROPE]
        reps = n_pe // LANES
        qpe = (qpe * jnp.concatenate([cos] * reps, axis=1)
               + qq[:, n_nope + n_pe:] * jnp.concatenate([sin] * reps, axis=1))
    for hd in range(A_HEADS):
        q_ref[0, hd, :, :QK_NOPE] = qq[:, hd * QK_NOPE:(hd + 1) * QK_NOPE].astype(BF16)
        q_ref[0, hd, :, QK_NOPE:] = qpe[:, hd * QK_ROPE:(hd + 1) * QK_ROPE].astype(BF16)
    kv_ref[0] = _mm(ckv, wukv_ref[...]).astype(BF16)
    kper_ref[0] = kpe.astype(BF16)
    ab_ref[0] = p[:, _C_AB:_C_AB + LANES]
    if emit_cache:
        ckv_ref[0] = ckv
        kpe_ref[0] = p[:, _C_KPE:_C_KPE + QK_ROPE]


def _even_in(x, shift, scale, ln, w, conv, qg, kg, wuq, wukv, rope_tabs, per_batch_cond, emit_cache):
    b, l, d = x.shape
    tm = min(l, 256)
    hb = tm // CONV_HALO
    nb = l // CONV_HALO
    cidx = (lambda i, t: (i, 0, 0)) if per_batch_cond else (lambda i, t: (0, 0, 0))
    in_specs = [pl.BlockSpec((1, CONV_HALO, d), lambda i, t: (i, jnp.maximum(t * hb - 1, 0), 0)),
                pl.BlockSpec((1, tm, d), lambda i, t: (i, t, 0)),
                pl.BlockSpec((1, CONV_HALO, d), lambda i, t: (i, jnp.minimum((t + 1) * hb, nb - 1), 0)),
                pl.BlockSpec((1, 1, d), cidx),
                pl.BlockSpec((1, 1, d), cidx),
                _resident((1, d)), _resident(w.shape), _resident(conv.shape), _resident((1, D_CQ)),
                _resident((1, D_CKV)), _resident(wuq.shape), _resident(wukv.shape)]
    args = [x, x, x, shift, scale, ln, w, conv, qg, kg, wuq, wukv]
    if rope_tabs is not None:
        in_specs += [pl.BlockSpec((tm, LANES), lambda i, t: (t, 0))] * 2
        args += list(rope_tabs)
    kvw = A_HEADS * (QK_NOPE + V_HEAD)
    out_shape = [jax.ShapeDtypeStruct((b, A_HEADS, l, QK_NOPE + QK_ROPE), BF16),
                 jax.ShapeDtypeStruct((b, l, kvw), BF16),
                 jax.ShapeDtypeStruct((b, l, QK_ROPE), BF16),
                 jax.ShapeDtypeStruct((b, l, QKV_W), F32),
                 jax.ShapeDtypeStruct((b, l, LANES), F32)]
    out_specs = [pl.BlockSpec((1, A_HEADS, tm, QK_NOPE + QK_ROPE), lambda i, t: (i, 0, t, 0)),
                 pl.BlockSpec((1, tm, kvw), lambda i, t: (i, t, 0)),
                 pl.BlockSpec((1, tm, QK_ROPE), lambda i, t: (i, t, 0)),
                 pl.BlockSpec((1, tm, QKV_W), lambda i, t: (i, t, 0)),
                 pl.BlockSpec((1, tm, LANES), lambda i, t: (i, t, 0))]
    if emit_cache:
        out_shape += [jax.ShapeDtypeStruct((b, l, D_CKV), F32), jax.ShapeDtypeStruct((b, l, QK_ROPE), F32)]
        out_specs += [pl.BlockSpec((1, tm, D_CKV), lambda i, t: (i, t, 0)),
                      pl.BlockSpec((1, tm, QK_ROPE), lambda i, t: (i, t, 0))]
    return pl.pallas_call(
        functools.partial(_even_in_kernel, rope_tabs is not None, emit_cache, l, tm),
        out_shape=out_shape, grid=(b, l // tm), in_specs=in_specs, out_specs=out_specs,
        compiler_params=_params("parallel", "parallel"),
        name="even_in",
    )(*args)


def _kv_up_kernel(c_ref, w_ref, o_ref):
    o_ref[0] = _mm(c_ref[0], w_ref[...]).astype(BF16)


def _kv_up(ckv, wukv):
    b, p, c = ckv.shape
    e = wukv.shape[1]
    return pl.pallas_call(
        _kv_up_kernel,
        out_shape=jax.ShapeDtypeStruct((b, p, e), BF16),
        grid=(b,),
        in_specs=[pl.BlockSpec((1, p, c), lambda i: (i, 0, 0)), _resident(wukv.shape)],
        out_specs=pl.BlockSpec((1, p, e), lambda i: (i, 0, 0)),
        compiler_params=_params("parallel"),
        name="kv_up",
    )(ckv, wukv)


def _attn_kernel(n_ctx, hb, th, *refs):
    if n_ctx:
        q_ref, kvc_ref, pec_ref, kvl_ref, pel_ref, o_ref, kcat, vcat = refs
    else:
        q_ref, kvl_ref, pel_ref, o_ref, kcat, vcat = refs
    hw = QK_NOPE + V_HEAD

    @pl.when(pl.program_id(2) == 0)
    def _():
        for hd in range(hb):
            if n_ctx:
                kcat[hd, :n_ctx, :QK_NOPE] = kvc_ref[0, :, hd * hw:hd * hw + QK_NOPE]
                kcat[hd, :n_ctx, QK_NOPE:] = pec_ref[0].astype(BF16)
                vcat[hd, :n_ctx, :] = kvc_ref[0, :, hd * hw + QK_NOPE:(hd + 1) * hw]
            kcat[hd, n_ctx:, :QK_NOPE] = kvl_ref[0, :, hd * hw:hd * hw + QK_NOPE]
            kcat[hd, n_ctx:, QK_NOPE:] = pel_ref[0]
            vcat[hd, n_ctx:, :] = kvl_ref[0, :, hd * hw + QK_NOPE:(hd + 1) * hw]

    c = (QK_NOPE + QK_ROPE) ** -0.5 * math.log2(math.e)
    blocks = [(hd, j) for hd in range(hb) for j in range(q_ref.shape[2] // th)]

    def scores(hd, j):
        return _mm_nt(q_ref[0, hd, j * th:(j + 1) * th, :], kcat[hd])

    ahead = 2
    pending = [scores(*blk) for blk in blocks[:ahead]]
    for i, (hd, j) in enumerate(blocks):
        s = pending.pop(0)
        e = jnp.exp2((s - jnp.max(s, axis=-1, keepdims=True)) * c)
        o = _mm(e, vcat[hd])
        if i + ahead < len(blocks):
            pending.append(scores(*blocks[i + ahead]))
        o_ref[0, j * th:(j + 1) * th, hd * V_HEAD:(hd + 1) * V_HEAD] = o * (1.0 / jnp.sum(e, axis=-1, keepdims=True))


def _attention(q, kv, kpe, ctx=None):
    b, hh, l, dq = q.shape
    th = min(l, ATTN_BLOCK)
    tq = min(l, ATTN_TILE)
    hb = min(hh, ATTN_TILE // tq)
    hw = QK_NOPE + V_HEAD
    n_ctx = 0 if ctx is None else ctx[0].shape[1]
    lk = n_ctx + l
    in_specs = [pl.BlockSpec((1, hb, tq, dq), lambda i, h, t: (i, h, t, 0))]
    args = [q]
    if ctx is not None:
        in_specs += [pl.BlockSpec((1, n_ctx, hb * hw), lambda i, h, t: (i, 0, h)),
                     pl.BlockSpec((1, n_ctx, QK_ROPE), lambda i, h, t: (i, 0, 0))]
        args += list(ctx)
    in_specs += [pl.BlockSpec((1, l, hb * hw), lambda i, h, t: (i, 0, h)),
                 pl.BlockSpec((1, l, QK_ROPE), lambda i, h, t: (i, 0, 0))]
    args += [kv, kpe]
    return pl.pallas_call(
        functools.partial(_attn_kernel, n_ctx, hb, th),
        out_shape=jax.ShapeDtypeStruct((b, l, hh * V_HEAD), F32),
        grid=(b, hh // hb, l // tq),
        in_specs=in_specs,
        out_specs=pl.BlockSpec((1, tq, hb * V_HEAD), lambda i, h, t: (i, t, h)),
        scratch_shapes=[pltpu.VMEM((hb, lk, dq), BF16), pltpu.VMEM((hb, lk, V_HEAD), BF16)],
        compiler_params=_params("parallel", "parallel", "arbitrary"),
        name="attention",
    )(*args)


def _unit_triangular_inverses(mats, rowi, coli, second, out):
    def diag(y):
        return jnp.concatenate([jnp.where(second, 0.0, y), jnp.where(second, y, 0.0)], axis=0)

    eye = (rowi == coli).astype(F32)
    same = (rowi // INV_BLOCK) == (coli // INV_BLOCK)
    ps = [jnp.where(same, a, 0.0) for a in mats]
    ts = [eye - p for p in ps]
    for _ in range(int(math.log2(INV_BLOCK)) - 1):
        ps = [_mm(p, diag(p)) for p in ps]
        yield
        ts = [t + _mm(t, diag(p)) for t, p in zip(ts, ps)]
        yield
    size = INV_BLOCK
    while size < CHUNK:
        wider = (rowi // (2 * size)) == (coli // (2 * size))
        pick = jnp.logical_and(wider, jnp.logical_not(same))
        cts = [_mm(jnp.where(pick, a, 0.0), diag(t)) for a, t in zip(mats, ts)]
        yield
        ts = [t - _mm(t, diag(ct)) for t, ct in zip(ts, cts)]
        yield
        same = wider
        size *= 2
    out.extend(ts)


def _interleave(*generators):
    live = list(generators)
    while live:
        for g in list(live):
            try:
                next(g)
            except StopIteration:
                live.remove(g)


def _two_blocks(top, bottom):
    zeros = jnp.zeros_like(top)
    return jnp.concatenate([jnp.concatenate([top, zeros], axis=1), jnp.concatenate([zeros, bottom], axis=1)], axis=0)


def _gdn_kernel(has_state, l, hb, group, qr_ref, kr_ref, vr_ref, abc_ref, abr_ref, alog_ref, dtb_ref, on_ref, *rest):
    if has_state:
        s0f_ref, s0b_ref, o_ref = rest[:3]
        scr = rest[3:]
        sf_ref = sb_ref = None
    else:
        o_ref, sf_ref, sb_ref = rest[:3]
        scr = rest[3:]
    (km_f, nm_f, qp_f, o_f, dc_f, km_b, nm_b, qp_b, o_b, dc_b) = scr
    per_dir = ((km_f, nm_f, qp_f, o_f, dc_f), (km_b, nm_b, qp_b, o_b, dc_b))
    n = l // CHUNK
    rb = min(l, 512)
    heads = range(hb)

    neg_a = [[-jnp.exp(alog_ref[hd, d]) for d in (0, 1)] for hd in heads]
    dt = [[dtb_ref[hd, d] for d in (0, 1)] for hd in heads]
    rowi = lax.broadcasted_iota(jnp.int32, (CHUNK, 2 * CHUNK), 0)
    lane = lax.broadcasted_iota(jnp.int32, (CHUNK, 2 * CHUNK), 1)
    bwd = lane >= CHUNK
    coli = jnp.where(bwd, lane - CHUNK, lane)
    ahead = jnp.where(bwd, coli - rowi, rowi - coli)
    tri = ahead >= 0
    tri_t = ahead <= 0

    def load_pair(hd, c):
        lanes = slice(hd * DK, (hd + 1) * DK)
        halves = []
        for d, cc in ((0, c), (1, n - 1 - c)):
            r0 = pl.multiple_of(cc * CHUNK, CHUNK)
            halves.append((qr_ref[0, pl.ds(r0, CHUNK), lanes], kr_ref[0, pl.ds(r0, CHUNK), lanes],
                           vr_ref[0, pl.ds(r0, CHUNK), lanes],
                           abc_ref[0, hd, pl.ds(r0, CHUNK), :],
                           abr_ref[0, hd, d, pl.ds(cc, 1), :]))
        return halves

    def gates(hd, halves):
        ks = [h[1] for h in halves]
        vs = [h[2] for h in halves]
        g_col = [neg_a[hd][d] * _softplus(halves[d][3][:, d:d + 1] + dt[hd][d]) for d in (0, 1)]
        g_row = jnp.concatenate([neg_a[hd][d] * _softplus(halves[d][4] + dt[hd][d]) for d in (0, 1)], axis=1)
        beta = [_sigmoid(halves[d][3][:, 2 + d:3 + d]) for d in (0, 1)]
        terms = jnp.where(tri, jnp.broadcast_to(g_row, (CHUNK, 2 * CHUNK)), 0.0)
        gc = [jnp.sum(jnp.where(bwd, 0.0, terms), axis=1, keepdims=True),
              jnp.sum(jnp.where(bwd, terms, 0.0), axis=1, keepdims=True)]
        gr = jnp.sum(jnp.where(tri_t, jnp.where(bwd, g_col[1], g_col[0]), 0.0), axis=0, keepdims=True)
        gamma = jnp.exp(jnp.where(tri, jnp.where(bwd, gc[1], gc[0]) - gr, -jnp.inf))
        kb = [ks[d] * beta[d] for d in (0, 1)]
        e_gc = [jnp.exp(gc[d]) for d in (0, 1)]
        g_end = [gc[0][CHUNK - 1:CHUNK, :], gc[1][0:1, :]]
        return dict(kb=kb, gamma=gamma, strict=jnp.where(rowi == coli, 0.0, gamma), e_gc=e_gc,
                    rhs=[jnp.concatenate([vs[d] * beta[d], kb[d] * e_gc[d]], axis=1) for d in (0, 1)],
                    kd=[(ks[d] * jnp.exp(g_end[d] - gc[d])).astype(BF16) for d in (0, 1)],
                    dec=[jnp.broadcast_to(jnp.exp(g_end[d]), (8, DV)) for d in (0, 1)])

    def chunk_group(i):
        where = [(hd, i * group + j) for hd in heads for j in range(group)]
        loaded = [load_pair(hd, c) for hd, c in where]
        gs = [gates(hd, hv) for (hd, _), hv in zip(where, loaded)]
        stacked = [_mm_nt(jnp.concatenate([jnp.concatenate([hv[0][0], hv[1][0]], axis=1),
                                           jnp.concatenate(g["kb"], axis=1)], axis=0),
                          _two_blocks(hv[0][1], hv[1][1])) for hv, g in zip(loaded, gs)]
        yield
        invs = []
        yield from _unit_triangular_inverses([st[CHUNK:] * g["strict"] for st, g in zip(stacked, gs)],
                                             rowi, coli, bwd, invs)
        sols = [_mm(t, _two_blocks(*g["rhs"])).astype(BF16) for t, g in zip(invs, gs)]
        yield
        sold = [[s[:, :2 * DV], s[:, 2 * DV:]] for s in sols]
        in_sol = [_mm(st[:CHUNK] * g["gamma"], _two_blocks(*sd)) for st, g, sd in zip(stacked, gs, sold)]
        yield
        kd_sol = [[_mm_tn(g["kd"][d], sd[d]) for d in (0, 1)] for g, sd in zip(gs, sold)]
        yield
        for (hd, c), hv, g, is_, ks_ in zip(where, loaded, gs, in_sol, kd_sol):
            for d, cc in ((0, c), (1, n - 1 - c)):
                km_s, nm_s, qp_s, o_s, dc_s = per_dir[d]
                r0 = pl.multiple_of(cc * CHUNK, CHUNK)
                m0 = pl.multiple_of(cc * DK, DK)
                nm_s[hd, pl.ds(m0, DK), :] = ks_[d][:, :DV]
                km_s[hd, pl.ds(m0, DK), :] = ks_[d][:, DV:].astype(BF16)
                o_s[hd, pl.ds(r0, CHUNK), :] = is_[:, 2 * d * DV:(2 * d + 1) * DV]
                qp_s[hd, pl.ds(r0, CHUNK), :] = (hv[d][0] * g["e_gc"][d]
                                                 - is_[:, (2 * d + 1) * DV:(2 * d + 2) * DV]).astype(BF16)
                dc_s[hd, pl.ds(pl.multiple_of(cc * 8, 8), 8), :] = g["dec"][d]

    def load_step(hd, d, c):
        km_s, nm_s, qp_s, o_s, dc_s = per_dir[d]
        r0 = pl.multiple_of(c * CHUNK, CHUNK)
        m0 = pl.multiple_of(c * DK, DK)
        return (km_s[hd, pl.ds(m0, DK), :], nm_s[hd, pl.ds(m0, DK), :], qp_s[hd, pl.ds(r0, CHUNK), :],
                o_s[hd, pl.ds(r0, CHUNK), :], dc_s[hd, pl.ds(pl.multiple_of(c * 8, 8), 8), :][0:1, :])

    chains = [(hd, d) for hd in heads for d in (0, 1)]

    def scan_step(c, states):
        cs = [c if d == 0 else n - 1 - c for _, d in chains]
        loaded = [load_step(hd, d, cc) for (hd, d), cc in zip(chains, cs)]
        sbs = [s.astype(BF16) for s in states]
        new = [s * dec + nm - _mm(km, sb) for (km, nm, qp, o_loc, dec), s, sb in zip(loaded, states, sbs)]
        outs = [o_loc + _mm(qp, sb) for (km, nm, qp, o_loc, dec), sb in zip(loaded, sbs)]
        for (hd, d), cc, out in zip(chains, cs, outs):
            per_dir[d][3][hd, pl.ds(pl.multiple_of(cc * CHUNK, CHUNK), CHUNK), :] = out
        return tuple(new)

    def scan_group(first, holder):
        for j in range(group):
            holder[0] = scan_step(first + j, holder[0])
            yield

    if has_state:
        init = tuple((s0f_ref, s0b_ref)[d][0, hd] for hd, d in chains)
    else:
        init = tuple(jnp.zeros((DK, DV), F32) for _ in chains)

    n_groups = n // group
    _interleave(chunk_group(0))

    def pipelined(g, states):
        holder = [states]
        _interleave(chunk_group(g), scan_group((g - 1) * group, holder))
        return holder[0]

    holder = [lax.fori_loop(1, n_groups, pipelined, init)]
    _interleave(scan_group((n_groups - 1) * group, holder))
    final = holder[0]
    if not has_state:
        for (hd, d), s in zip(chains, final):
            (sf_ref, sb_ref)[d][0, hd] = s

    def out_block(i, carry):
        r0 = pl.multiple_of(i * rb, rb)
        for hd in heads:
            o_ref[0, pl.ds(r0, rb), hd * DV:(hd + 1) * DV] = _rms(
                o_f[hd, pl.ds(r0, rb), :] + o_b[hd, pl.ds(r0, rb), :], on_ref[...])
        return carry

    lax.fori_loop(0, l // rb, out_block, 0)


def _gdn(qkv, ab, a_log, dt_bias, o_norm, states):
    b, l, _ = qkv.shape
    n = l // CHUNK
    hh = B_HEADS
    ab4 = jnp.transpose(ab[:, :, :AB_W].reshape(b, l, 4, hh), (0, 3, 1, 2))
    abr = jnp.transpose(ab[:, :, :AB_W].reshape(b, n, CHUNK, 4, hh), (0, 4, 3, 1, 2))
    alog = jnp.transpose(a_log).reshape(hh, 2, 1, 1)
    dtb = jnp.transpose(dt_bias).reshape(hh, 2, 1, 1)
    has_state = states is not None
    hb = max(1, min(hh, GDN_GROUP // n))
    hg = hh // hb
    in_specs = [pl.BlockSpec((1, l, hb * DK), lambda i, h: (i, 0, h)),
                pl.BlockSpec((1, l, hb * DK), lambda i, h: (i, 0, hg + h)),
                pl.BlockSpec((1, l, hb * DV), lambda i, h: (i, 0, 2 * hg + h)),
                pl.BlockSpec((1, hb, l, 4), lambda i, h: (i, h, 0, 0)),
                pl.BlockSpec((1, hb, 4, n, CHUNK), lambda i, h: (i, h, 0, 0, 0)),
                pl.BlockSpec((hb, 2, 1, 1), lambda i, h: (h, 0, 0, 0)),
                pl.BlockSpec((hb, 2, 1, 1), lambda i, h: (h, 0, 0, 0)),
                pl.BlockSpec((1, DV), lambda i, h: (0, 0))]
    args = [qkv, qkv, qkv, ab4, abr, alog, dtb, o_norm.reshape(1, DV)]
    st_spec = pl.BlockSpec((1, hb, DK, DV), lambda i, h: (i, h, 0, 0))
    out_shape = [jax.ShapeDtypeStruct((b, l, hh * DV), F32)]
    out_specs = [pl.BlockSpec((1, l, hb * DV), lambda i, h: (i, 0, h))]
    if has_state:
        in_specs += [st_spec, st_spec]
        args += list(states)
    else:
        out_shape += [jax.ShapeDtypeStruct((b, hh, DK, DV), F32)] * 2
        out_specs += [st_spec, st_spec]
    per_dir = [pltpu.VMEM((hb, n * DK, DV), BF16), pltpu.VMEM((hb, n * DK, DV), F32),
               pltpu.VMEM((hb, l, DK), BF16), pltpu.VMEM((hb, l, DV), F32), pltpu.VMEM((hb, n * 8, DV), F32)]
    scratch = per_dir + per_dir
    group = math.gcd(n, max(1, GDN_GROUP // hb))
    return pl.pallas_call(
        functools.partial(_gdn_kernel, has_state, l, hb, group),
        out_shape=out_shape, grid=(b, hg), in_specs=in_specs, out_specs=out_specs,
        scratch_shapes=scratch,
        compiler_params=_params("parallel", "parallel"),
        name="gated_deltanet",
    )(*args)


def _even_out_kernel(x_ref, sh_ref, sc_ref, gt_ref, ln_ref, at_ref, gd_ref, wz_ref, wo_ref, o_ref):
    x = x_ref[0]
    z = _mm(_adaln(x, ln_ref[...], sc_ref[0], sh_ref[0]), wz_ref[...])
    y = jnp.concatenate([at_ref[0], gd_ref[0]], axis=1) * _silu(z)
    o_ref[0] = x + gt_ref[0] * _mm(y, wo_ref[...])


def _even_out(x, shift, scale, gate, ln, attn, gdn, wz, wo, per_batch_cond):
    b, l, d = x.shape
    tm = min(l, 256)
    cidx = (lambda i, t: (i, 0, 0)) if per_batch_cond else (lambda i, t: (0, 0, 0))
    row = lambda i, t: (i, t, 0)
    return pl.pallas_call(
        _even_out_kernel,
        out_shape=jax.ShapeDtypeStruct((b, l, d), F32),
        grid=(b, l // tm),
        in_specs=[pl.BlockSpec((1, tm, d), row), pl.BlockSpec((1, 1, d), cidx), pl.BlockSpec((1, 1, d), cidx),
                  pl.BlockSpec((1, 1, d), cidx), _resident((1, d)),
                  pl.BlockSpec((1, tm, attn.shape[2]), row), pl.BlockSpec((1, tm, gdn.shape[2]), row),
                  _resident(wz.shape), _resident(wo.shape)],
        out_specs=pl.BlockSpec((1, tm, d), row),
        compiler_params=_params("parallel", "parallel"),
        name="even_out",
    )(x, shift, scale, gate, ln, attn, gdn, wz, wo)


def _pool_kernel(final, l, tm, xp_ref, x_ref, xn_ref, sh_ref, sc_ref, gt_ref, ln_ref, wp_ref, wz_ref,
                 wg_ref, ps_ref, wo_ref, fn_ref, o_ref):
    t = pl.program_id(1)
    x = x_ref[0]
    rows = tm + 2 * POOL_HALO
    xa = jnp.concatenate([xp_ref[0], x, xn_ref[0]], axis=0)
    h = _adaln(xa, ln_ref[...], sc_ref[0], sh_ref[0])
    ri = lax.broadcasted_iota(jnp.int32, (rows, 1), 0) + (t * tm - POOL_HALO)
    inside = jnp.logical_and(ri >= 0, ri < l)
    pin = jnp.where(inside, _mm(h, wp_ref[...]), 0.0)
    z = _mm(h[POOL_HALO:POOL_HALO + tm], wz_ref[...])
    pos = lax.broadcasted_iota(jnp.int32, (tm, 1), 0) + t * tm
    gw = pin.shape[1] // len(POOL_WINDOWS)
    mixed = []
    for gi, w in enumerate(POOL_WINDOWS):
        pg = pin[:, gi * gw:(gi + 1) * gw]
        acc = pg
        span = 1
        while span < w:
            acc = acc + pltpu.roll(acc, rows - span, 0)
            span *= 2
        first = POOL_HALO - w // 2
        win = pltpu.roll(acc, rows - first, 0)[:tm] if first else acc[:tm]
        cnt = (jnp.minimum(pos + (w - w // 2), l) - jnp.maximum(pos - w // 2, 0)).astype(F32)
        pooled = win / cnt - pg[POOL_HALO:POOL_HALO + tm]
        mixed.append(_mm(pooled, wg_ref[gi]))
    y = jnp.concatenate(mixed, axis=1) * ps_ref[...] * _silu(z)
    out = x + gt_ref[0] * _mm(y, wo_ref[...])
    o_ref[0] = _rms(out, fn_ref[...]) if final else out


def _pool_layer(x, shift, scale, gate, ln, wp, wz, wg, ps, wo, fnorm, per_batch_cond, final):
    b, l, d = x.shape
    tm = min(l, 256)
    hb = tm // POOL_HALO
    nb = l // POOL_HALO
    cidx = (lambda i, t: (i, 0, 0)) if per_batch_cond else (lambda i, t: (0, 0, 0))
    row = lambda i, t: (i, t, 0)
    return pl.pallas_call(
        functools.partial(_pool_kernel, final, l, tm),
        out_shape=jax.ShapeDtypeStruct((b, l, d), F32),
        grid=(b, l // tm),
        in_specs=[pl.BlockSpec((1, POOL_HALO, d), lambda i, t: (i, jnp.maximum(t * hb - 1, 0), 0)),
                  pl.BlockSpec((1, tm, d), row),
                  pl.BlockSpec((1, POOL_HALO, d), lambda i, t: (i, jnp.minimum((t + 1) * hb, nb - 1), 0)),
                  pl.BlockSpec((1, 1, d), cidx), pl.BlockSpec((1, 1, d), cidx), pl.BlockSpec((1, 1, d), cidx),
                  _resident((1, d)), _resident(wp.shape), _resident(wz.shape), _resident(wg.shape),
                  _resident((1, d)), _resident(wo.shape), _resident((1, d))],
        out_specs=pl.BlockSpec((1, tm, d), row),
        compiler_params=_params("parallel", "parallel"),
        name="pool_mixer",
    )(x, x, x, shift, scale, gate, ln, wp, wz, wg, ps, wo, fnorm)


def _rope_tables(n_tokens):
    rows = n_tokens // GRID_W
    row_pos = jnp.repeat(jnp.arange(rows), GRID_W).astype(F32)
    col_pos = jnp.tile(jnp.arange(GRID_W), rows).astype(F32)
    half = QK_ROPE // 2
    inv_freq = ROPE_BASE ** (-jnp.arange(0, half, 2, dtype=F32) / half)
    ar = row_pos[:, None] * inv_freq
    ac = col_pos[:, None] * inv_freq
    cos = jnp.concatenate([jnp.cos(ar), jnp.cos(ar), jnp.cos(ac), jnp.cos(ac)], axis=1)
    sin = jnp.concatenate([-jnp.sin(ar), jnp.sin(ar), -jnp.sin(ac), jnp.sin(ac)], axis=1)
    return jnp.concatenate([cos, cos], axis=1), jnp.concatenate([sin, sin], axis=1)


def _swap_perm():
    q = QK_ROPE // 4
    return jnp.array(list(range(q, 2 * q)) + list(range(q)) + list(range(3 * q, 4 * q)) + list(range(2 * q, 3 * q)))


def _even_weights(w_in, w_uq, w_ukv, w_out):
    d = w_in.shape[0]
    perm = _swap_perm()
    o = 0
    cq = w_in[:, o:o + D_CQ]; o += D_CQ
    ckv = w_in[:, o:o + D_CKV]; o += D_CKV
    kpe = w_in[:, o:o + QK_ROPE]; o += QK_ROPE
    qkv = w_in[:, o:o + QKV_W]; o += QKV_W
    ab = w_in[:, o:o + AB_W]; o += AB_W
    wz = w_in[:, o:]
    pad = lambda n: jnp.zeros((d, n), w_in.dtype)
    w = jnp.concatenate([cq, ckv, kpe, pad(LANES - QK_ROPE), kpe[:, perm], pad(LANES - QK_ROPE),
                         ab, pad(LANES - AB_W), qkv], axis=1).astype(BF16)
    uq = w_uq.reshape(D_CQ, A_HEADS, QK_NOPE + QK_ROPE)
    pe = uq[:, :, QK_NOPE:]
    wuq = jnp.concatenate([uq[:, :, :QK_NOPE].reshape(D_CQ, -1), pe.reshape(D_CQ, -1),
                           pe[:, :, perm].reshape(D_CQ, -1)], axis=1).astype(BF16)
    return w, wuq, w_ukv.astype(BF16), wz.astype(BF16), w_out.astype(BF16)


def kernel(x_prompt, x_sample, cache_ckv, cache_kpe, state_fwd, state_bwd, c, c_ctx, ln_e, mod_w_e, mod_b_e, w_in_e, q_norm_e, kv_norm_e, w_uq_e, w_ukv_e, conv_e, a_log_e, dt_bias_e, o_norm_e, w_out_e, ln_o, mod_w_o, mod_b_o, w_in_o, w_pool_o, pool_scale_o, w_out_o, final_norm):
    d = x_prompt.shape[-1]
    nb = c.shape[0]
    depth = ln_e.shape[0] + ln_o.shape[0]
    assert depth % 2 == 0, "the final norm is fused into the last (pooling) layer"
    cond = jnp.concatenate([c_ctx[None, :], c, jnp.zeros((COND_ROWS - 1 - nb, d), F32)], axis=0)
    mod = {0: _modulation(cond, mod_w_e, mod_b_e), 1: _modulation(cond, mod_w_o, mod_b_o)}
    rope = _rope_tables(x_sample.shape[1])
    fnorm = final_norm.reshape(1, d)
    xp, xs = x_prompt, x_sample
    ckv_out, kpe_out, sf_out, sb_out = [], [], [], []
    for layer in range(depth):
        i = layer // 2
        m = mod[layer % 2][i]
        sh_p, sc_p, gt_p = (m[0:1, j * d:(j + 1) * d].reshape(1, 1, d) for j in range(3))
        sh_s, sc_s, gt_s = (m[1:1 + nb, j * d:(j + 1) * d].reshape(nb, 1, d) for j in range(3))
        if layer % 2 == 0:
            ln = ln_e[i].reshape(1, d)
            w, wuq, wukv, wz, wo = _even_weights(w_in_e[i], w_uq_e[i], w_ukv_e[i], w_out_e[i])
            qg, kg = q_norm_e[i].reshape(1, D_CQ), kv_norm_e[i].reshape(1, D_CKV)
            gdn_w = (a_log_e[i], dt_bias_e[i], o_norm_e[i])
            in_w = (ln, w, conv_e[i], qg, kg, wuq, wukv)
            q, kv, kper, qkv, ab, ckv, kpe = _even_in(xp, sh_p, sc_p, *in_w, None, False, True)
            attn = _attention(q, kv, kper)
            g_o, s_f, s_b = _gdn(qkv, ab, *gdn_w, None)
            xp = _even_out(xp, sh_p, sc_p, gt_p, ln, attn, g_o, wz, wo, False)
            ckv_out.append(ckv); kpe_out.append(kpe); sf_out.append(s_f); sb_out.append(s_b)
            q, kv, kper, qkv, ab = _even_in(xs, sh_s, sc_s, *in_w, rope, True, False)
            attn = _attention(q, kv, kper, (_kv_up(cache_ckv[:, i], wukv), cache_kpe[:, i]))
            (g_o,) = _gdn(qkv, ab, *gdn_w, (state_fwd[:, i], state_bwd[:, i]))
            xs = _even_out(xs, sh_s, sc_s, gt_s, ln, attn, g_o, wz, wo, True)
        else:
            ln = ln_o[i].reshape(1, d)
            wp = w_in_o[i][:, :d].astype(BF16)
            wzo = w_in_o[i][:, d:].astype(BF16)
            wg = w_pool_o[i].astype(BF16)
            ps = pool_scale_o[i].reshape(1, d)
            wo = w_out_o[i].astype(BF16)
            final = layer == depth - 1
            xp = _pool_layer(xp, sh_p, sc_p, gt_p, ln, wp, wzo, wg, ps, wo, fnorm, False, final)
            xs = _pool_layer(xs, sh_s, sc_s, gt_s, ln, wp, wzo, wg, ps, wo, fnorm, True, final)
    return (xp, xs, jnp.stack(ckv_out, axis=1), jnp.stack(kpe_out, axis=1),
            jnp.stack(sf_out, axis=1), jnp.stack(sb_out, axis=1))
```
